```python
import math
import jax, jax.numpy as jnp
from jax import lax
import numpy as np

D_MODEL = 4096
BATCH = 4
SEQ = 2048
DEPTH = 2
DEC_BATCH = 8
DEC_SEQ = 4
PAST_LEN = 16384
PAGE_SIZE = 128

N_MIXERS = 2
N_ATTN_LAYERS = (DEPTH + 1) // 2
N_RET_LAYERS = DEPTH // 2
DIFF_HEAD_DIM = 128
DIFF_HEADS = D_MODEL // (2 * DIFF_HEAD_DIM)
DIFF_SUB = 2 * DIFF_HEADS
DIFF_V_DIM = 2 * DIFF_HEAD_DIM
DIFF_Q_W = DIFF_SUB * DIFF_HEAD_DIM
DIFF_K_W = DIFF_SUB * DIFF_HEAD_DIM
DIFF_V_W = DIFF_HEADS * DIFF_V_DIM
Q_BLOCK = 128
RET_HEADS = 16
RET_KEY_DIM = D_MODEL // RET_HEADS
RET_VAL_DIM = 2 * RET_KEY_DIM
RET_QK_W = RET_HEADS * RET_KEY_DIM
RET_V_W = RET_HEADS * RET_VAL_DIM
RET_CHUNK = 128
RET_ROT_BASE = 10000.0
D_FF = 11008
CONV_W = 3
NORM_EPS = 1e-6

kernel_name = 'diffattn_retnet_convffn_hybrid_step'


def _rmsnorm(x, w):
    xf = x.astype(jnp.float32)
    y = xf * lax.rsqrt(jnp.mean(xf * xf, axis=-1, keepdims=True) + NORM_EPS)
    return (y * w.astype(jnp.float32)).astype(x.dtype)


def _lambda_init(layer_idx):
    return 0.8 - 0.6 * math.exp(-0.3 * layer_idx)


def _diff_lambda(lq1, lk1, lq2, lk2, lam_init):
    f = jnp.float32
    return (jnp.exp(jnp.sum(lq1.astype(f) * lk1.astype(f)))
            - jnp.exp(jnp.sum(lq2.astype(f) * lk2.astype(f))) + lam_init)


def _diff_qkv(h, w_qkv, q_norm_w, k_norm_w):
    b, s, _ = h.shape
    q, k, v = jnp.split(h @ w_qkv, [DIFF_Q_W, DIFF_Q_W + DIFF_K_W], axis=-1)
    q = _rmsnorm(q.reshape(b, s, DIFF_SUB, DIFF_HEAD_DIM), q_norm_w)
    k = _rmsnorm(k.reshape(b, s, DIFF_SUB, DIFF_HEAD_DIM), k_norm_w)
    v = v.reshape(b, s, DIFF_HEADS, DIFF_V_DIM)
    return q, k, v


def _diff_attn_prompt(q, k, v, lam):
    b, s = q.shape[:2]
    nqb = s // Q_BLOCK
    scale = DIFF_HEAD_DIM ** -0.5
    qb = q.reshape(b, nqb, Q_BLOCK, DIFF_SUB, DIFF_HEAD_DIM).swapaxes(0, 1)
    kf = k.astype(jnp.float32)
    vf = v.astype(jnp.float32)
    kpos = jnp.arange(s)

    def block(args):
        qi, i = args
        sc = jnp.einsum('bqhd,bkhd->bhqk', qi.astype(jnp.float32) * scale, kf)
        qpos = i * Q_BLOCK + jnp.arange(Q_BLOCK)
        sc = jnp.where(kpos[None, :] <= qpos[:, None], sc, -jnp.inf)
        p = jax.nn.softmax(sc, axis=-1).reshape(b, DIFF_HEADS, 2, Q_BLOCK, s)
        a = p[:, :, 0] - lam * p[:, :, 1]
        return jnp.einsum('bhqk,bkhe->bqhe', a, vf)

    o = lax.map(block, (qb, jnp.arange(nqb)))
    return o.swapaxes(0, 1).reshape(b, s, DIFF_HEADS, DIFF_V_DIM)


def _diff_attn_sample(q, k_new, v_new, cache_k, cache_v, layer, page_table, lam):
    b, t = q.shape[:2]
    qf = q.astype(jnp.float32) * (DIFF_HEAD_DIM ** -0.5)

    def scores(kb):
        sc = jnp.einsum('bqhd,bkhd->bhqk', qf, kb.astype(jnp.float32))
        return sc.reshape(b, DIFF_HEADS, 2, t, kb.shape[1])

    def pv(p, vb):
        return jnp.einsum('bhcqk,bkhe->bhcqe', p, vb.astype(jnp.float32))

    causal = jnp.arange(t)[None, :] <= jnp.arange(t)[:, None]
    s0 = jnp.where(causal, scores(k_new), -jnp.inf)
    m0 = jnp.max(s0, axis=-1)
    p0 = jnp.exp(s0 - m0[..., None])
    carry = (m0, jnp.sum(p0, axis=-1), pv(p0, v_new))

    def step(carry, pages):
        m, l, acc = carry
        sc = scores(cache_k[layer, pages])
        m_new = jnp.maximum(m, jnp.max(sc, axis=-1))
        alpha = jnp.exp(m - m_new)
        p = jnp.exp(sc - m_new[..., None])
        l = l * alpha + jnp.sum(p, axis=-1)
        acc = acc * alpha[..., None] + pv(p, cache_v[layer, pages])
        return (m_new, l, acc), None

    (m, l, acc), _ = lax.scan(step, carry, page_table.T)
    o = acc / l[..., None]
    o = o[:, :, 0] - lam * o[:, :, 1]
    return o.transpose(0, 2, 1, 3)


def _diff_out(o, subln_w, lam_init, w_o):
    b, s = o.shape[:2]
    o = _rmsnorm(o, subln_w) * (1.0 - lam_init)
    return o.reshape(b, s, DIFF_V_W).astype(w_o.dtype) @ w_o


def _ret_log_decay():
    return jnp.log1p(-jnp.exp2(-5.0 - jnp.arange(RET_HEADS, dtype=jnp.float32)))


def _ret_rotate(x, pos):
    inv = 1.0 / (RET_ROT_BASE ** jnp.linspace(0.0, 1.0, RET_KEY_DIM // 2, dtype=jnp.float32))
    ang = pos.astype(jnp.float32)[:, None] * inv[None, :]
    cos = jnp.cos(ang)[None, :, None, :]
    sin = jnp.sin(ang)[None, :, None, :]
    xf = x.astype(jnp.float32).reshape(*x.shape[:-1], RET_KEY_DIM // 2, 2)
    x1, x2 = xf[..., 0], xf[..., 1]
    return jnp.stack([x1 * cos - x2 * sin, x2 * cos + x1 * sin], axis=-1).reshape(x.shape)


def _ret_project(h, pos, w_qkvg):
    b, s, _ = h.shape
    q, k, v, g = jnp.split(h @ w_qkvg, [RET_QK_W, 2 * RET_QK_W, 2 * RET_QK_W + RET_V_W], axis=-1)
    q = _ret_rotate(q.reshape(b, s, RET_HEADS, RET_KEY_DIM), pos)
    k = _ret_rotate(k.reshape(b, s, RET_HEADS, RET_KEY_DIM), pos) * (RET_KEY_DIM ** -0.5)
    v = v.reshape(b, s, RET_HEADS, RET_VAL_DIM).astype(jnp.float32)
    return q, k, v, g


def _ret_chunk(R, q, k, v, log_g):
    L = q.shape[1]
    idx = jnp.arange(L, dtype=jnp.float32)
    rel = idx[:, None] - idx[None, :]
    dmask = jnp.where(rel >= 0, jnp.exp(jnp.maximum(rel, 0.0)[None] * log_g[:, None, None]), 0.0)
    inner = jnp.einsum('bqhd,bkhd->bhqk', q, k) * dmask[None]
    o = jnp.einsum('bhqk,bkhe->bqhe', inner, v)
    cross_dec = jnp.exp((idx + 1.0)[:, None] * log_g[None, :])[None, :, :, None]
    o = o + jnp.einsum('bqhd,bhde->bqhe', q, R) * cross_dec
    kdec = k * jnp.exp((L - 1.0 - idx)[:, None] * log_g[None, :])[None, :, :, None]
    R = R * jnp.exp(L * log_g)[None, :, None, None] + jnp.einsum('bkhd,bkhe->bhde', kdec, v)
    return R, o


def _retention_prompt(q, k, v, log_g):
    b, s = q.shape[:2]
    nc = s // RET_CHUNK

    def chunks(x):
        return x.reshape(b, nc, RET_CHUNK, *x.shape[2:]).swapaxes(0, 1)

    R0 = jnp.zeros((b, RET_HEADS, RET_KEY_DIM, RET_VAL_DIM), jnp.float32)
    R, o = lax.scan(lambda R, xs: _ret_chunk(R, xs[0], xs[1], xs[2], log_g), R0,
                    (chunks(q), chunks(k), chunks(v)))
    return R, o.swapaxes(0, 1).reshape(b, s, RET_HEADS, RET_VAL_DIM)


def _ret_out(o, g, norm_w, w_o):
    b, s = o.shape[:2]
    y = _rmsnorm(o, norm_w).reshape(b, s, RET_V_W)
    return (jax.nn.silu(g.astype(jnp.float32)) * y).astype(w_o.dtype) @ w_o


def _conv_ffn(h, buf, w_up, conv_w, conv_b, w_down):
    u = h @ w_up
    s = u.shape[1]
    ext = jnp.concatenate([buf.astype(u.dtype), u], axis=1)
    c = conv_b + sum(ext[:, j:j + s] * conv_w[j] for j in range(CONV_W))
    gate, up = jnp.split(c, 2, axis=-1)
    return (jax.nn.silu(gate) * up) @ w_down, ext[:, -(CONV_W - 1):]


def setup_inputs(seed: int = 0) -> dict:
    key = jax.random.key(seed)
    ks = jax.random.split(key, 32)
    f = jnp.float32
    n_pages = PAST_LEN // PAGE_SIZE
    n_used = DEC_BATCH * n_pages
    n_pool = (n_used * 5) // 4

    def nrm(k, shape, scale):
        return jax.random.normal(k, shape, f) * scale

    def gain(k, shape):
        return 1.0 + 0.02 * jax.random.normal(k, shape, f)

    na, nr = N_ATTN_LAYERS, N_RET_LAYERS
    return {
        'x_prompt': nrm(ks[0], (BATCH, SEQ, D_MODEL), 1.0),
        'x_sample': nrm(ks[1], (DEC_BATCH, DEC_SEQ, D_MODEL), 1.0),
        'cache_k': nrm(ks[2], (na, n_pool, PAGE_SIZE, DIFF_SUB, DIFF_HEAD_DIM), 1.0),
        'cache_v': nrm(ks[3], (na, n_pool, PAGE_SIZE, DIFF_HEADS, DIFF_V_DIM), 1.0),
        'state_ret': nrm(ks[4], (nr, DEC_BATCH, RET_HEADS, RET_KEY_DIM, RET_VAL_DIM), 0.5),
        'state_ffn': nrm(ks[5], (DEPTH, DEC_BATCH, CONV_W - 1, 2 * D_FF), 1.0),
        'page_table': jax.random.permutation(ks[6], n_pool)[:n_used].reshape(DEC_BATCH, n_pages).astype(jnp.int32),
        'norm_mix_w': gain(ks[7], (DEPTH, D_MODEL)),
        'norm_ffn_w': gain(ks[8], (DEPTH, D_MODEL)),
        'w_attn_qkv': nrm(ks[9], (na, D_MODEL, DIFF_Q_W + DIFF_K_W + DIFF_V_W), D_MODEL ** -0.5),
        'q_norm_w': gain(ks[10], (na, DIFF_HEAD_DIM)),
        'k_norm_w': gain(ks[11], (na, DIFF_HEAD_DIM)),
        'lambda_q1': nrm(ks[12], (na, DIFF_HEAD_DIM), 0.1),
        'lambda_k1': nrm(ks[13], (na, DIFF_HEAD_DIM), 0.1),
        'lambda_q2': nrm(ks[14], (na, DIFF_HEAD_DIM), 0.1),
        'lambda_k2': nrm(ks[15], (na, DIFF_HEAD_DIM), 0.1),
        'subln_w': gain(ks[16], (na, DIFF_V_DIM)),
        'w_attn_o': nrm(ks[17], (na, DIFF_V_W, D_MODEL), DIFF_V_W ** -0.5),
        'w_ret_qkvg': nrm(ks[18], (nr, D_MODEL, 2 * RET_QK_W + 2 * RET_V_W), D_MODEL ** -0.5),
        'ret_norm_w': gain(ks[19], (nr, RET_VAL_DIM)),
        'w_ret_o': nrm(ks[20], (nr, RET_V_W, D_MODEL), RET_V_W ** -0.5),
        'w_ffn_up': nrm(ks[21], (DEPTH, D_MODEL, 2 * D_FF), D_MODEL ** -0.5),
        'ffn_conv_w': nrm(ks[22], (DEPTH, CONV_W, 2 * D_FF), CONV_W ** -0.5),
        'ffn_conv_b': nrm(ks[23], (DEPTH, 2 * D_FF), 0.02),
        'w_ffn_down': nrm(ks[24], (DEPTH, D_FF, D_MODEL), D_FF ** -0.5),
    }


def reference(x_prompt, x_sample, cache_k, cache_v, state_ret, state_ffn, page_table,
              norm_mix_w, norm_ffn_w, w_attn_qkv, q_norm_w, k_norm_w,
              lambda_q1, lambda_k1, lambda_q2, lambda_k2, subln_w, w_attn_o,
              w_ret_qkvg, ret_norm_w, w_ret_o, w_ffn_up, ffn_conv_w, ffn_conv_b, w_ffn_down):
    xp, xs = x_prompt, x_sample
    bp, sp = xp.shape[:2]
    bs, ss = xs.shape[:2]
    past = page_table.shape[1] * cache_k.shape[2]
    pos_p = jnp.arange(sp)
    pos_s = past + jnp.arange(ss)
    kp_l, vp_l, ks_l, vs_l, rp_l, rs_l, fp_l, fs_l = [], [], [], [], [], [], [], []
    for i in range(DEPTH):
        hp = _rmsnorm(xp, norm_mix_w[i])
        hs = _rmsnorm(xs, norm_mix_w[i])
        if i % N_MIXERS == 0:
            a = i // N_MIXERS
            lam_init = _lambda_init(i)
            lam = _diff_lambda(lambda_q1[a], lambda_k1[a], lambda_q2[a], lambda_k2[a], lam_init)
            qp, kp, vp = _diff_qkv(hp, w_attn_qkv[a], q_norm_w[a], k_norm_w[a])
            qs, kss, vss = _diff_qkv(hs, w_attn_qkv[a], q_norm_w[a], k_norm_w[a])
            op = _diff_attn_prompt(qp, kp, vp, lam)
            osm = _diff_attn_sample(qs, kss, vss, cache_k, cache_v, a, page_table, lam)
            xp = xp + _diff_out(op, subln_w[a], lam_init, w_attn_o[a]).astype(xp.dtype)
            xs = xs + _diff_out(osm, subln_w[a], lam_init, w_attn_o[a]).astype(xs.dtype)
            kp_l.append(kp)
            vp_l.append(vp)
            ks_l.append(kss)
            vs_l.append(vss)
        else:
            r = i // N_MIXERS
            log_g = _ret_log_decay()
            qp, kp, vp, gp = _ret_project(hp, pos_p, w_ret_qkvg[r])
            qs, kss, vss, gs = _ret_project(hs, pos_s, w_ret_qkvg[r])
            Rp, op = _retention_prompt(qp, kp, vp, log_g)
            Rs, osm = _ret_chunk(state_ret[r].astype(jnp.float32), qs, kss, vss, log_g)
            xp = xp + _ret_out(op, gp, ret_norm_w[r], w_ret_o[r]).astype(xp.dtype)
            xs = xs + _ret_out(osm, gs, ret_norm_w[r], w_ret_o[r]).astype(xs.dtype)
            rp_l.append(Rp.astype(state_ret.dtype))
            rs_l.append(Rs.astype(state_ret.dtype))
        hp = _rmsnorm(xp, norm_ffn_w[i])
        hs = _rmsnorm(xs, norm_ffn_w[i])
        buf0 = jnp.zeros((bp, CONV_W - 1, 2 * D_FF), xp.dtype)
        yp, bufp = _conv_ffn(hp, buf0, w_ffn_up[i], ffn_conv_w[i], ffn_conv_b[i], w_ffn_down[i])
        ys, bufs = _conv_ffn(hs, state_ffn[i], w_ffn_up[i], ffn_conv_w[i], ffn_conv_b[i], w_ffn_down[i])
        xp = xp + yp.astype(xp.dtype)
        xs = xs + ys.astype(xs.dtype)
        fp_l.append(bufp)
        fs_l.append(bufs)
    k_prompt = jnp.stack(kp_l)
    v_prompt = jnp.stack(vp_l)
    k_sample = jnp.stack(ks_l)
    v_sample = jnp.stack(vs_l)
    ret_state_prompt = jnp.stack(rp_l)
    ret_state_sample = jnp.stack(rs_l)
    ffn_state_prompt = jnp.stack(fp_l)
    ffn_state_sample = jnp.stack(fs_l)
    return (xp, xs, k_prompt, v_prompt, k_sample, v_sample,
            ret_state_prompt, ret_state_sample, ffn_state_prompt, ffn_state_sample)
```

```python
import functools
import math

import jax
import jax.numpy as jnp
from jax import lax
from jax.experimental import pallas as pl
from jax.experimental.pallas import tpu as pltpu

F32 = jnp.float32
BF16 = jnp.bfloat16
NORM_EPS = 1e-6
RET_ROT_BASE = 10000.0
LANES = 128
SUBLANES = 8
VMEM_LIMIT = 56 * 1024 * 1024
NEG = -1e30


def _dot(a, b):
    return jnp.dot(a, b, preferred_element_type=F32)


def _dot_nt(a, b):
    return lax.dot_general(a, b, (((1,), (1,)), ((), ())), preferred_element_type=F32)


def _dot_tn(a, b):
    return lax.dot_general(a, b, (((0,), (0,)), ((), ())), preferred_element_type=F32)


def _params(sem):
    return pltpu.CompilerParams(dimension_semantics=sem, vmem_limit_bytes=VMEM_LIMIT)


def _rms(x, w):
    return x * lax.rsqrt(jnp.mean(x * x, axis=-1, keepdims=True) + NORM_EPS) * w


def _rmsnorm_kernel(x_ref, w_ref, o_ref):
    o_ref[...] = _rms(x_ref[...], w_ref[...]).astype(o_ref.dtype)


def _rmsnorm(x, w_row, tr=256):
    rows, d = x.shape
    tr = min(tr, rows)
    return pl.pallas_call(
        _rmsnorm_kernel,
        grid=(rows // tr,),
        in_specs=[pl.BlockSpec((tr, d), lambda i: (i, 0)),
                  pl.BlockSpec((1, d), lambda i: (0, 0))],
        out_specs=pl.BlockSpec((tr, d), lambda i: (i, 0)),
        out_shape=jax.ShapeDtypeStruct((rows, d), BF16),
        compiler_params=_params(("parallel",)),
        name="rmsnorm",
    )(x, w_row)


def _mm_kernel(*refs, nk, n_extra, n_out, epilogue):
    a_ref, w_ref = refs[:2]
    extra = refs[2:2 + n_extra]
    outs = refs[2 + n_extra:2 + n_extra + n_out]
    part = _dot(a_ref[...], w_ref[...].astype(BF16))
    if nk == 1:
        epilogue(part, extra, outs)
        return
    acc_ref = refs[-1]
    k = pl.program_id(2)

    @pl.when(k == 0)
    def _():
        acc_ref[...] = part

    @pl.when(k > 0)
    def _():
        acc_ref[...] += part

    @pl.when(k == nk - 1)
    def _():
        epilogue(acc_ref[...], extra, outs)


def _matmul(a, w, layer, *, col0, ncols, out_dtypes, epilogue, extras=(), extra_specs=(),
            tm=1024, tn=512, tk=4096, name="matmul"):
    m, kdim = a.shape
    tm, tn, tk = min(tm, m), min(tn, ncols), min(tk, kdim)
    assert m % tm == 0 and ncols % tn == 0 and col0 % tn == 0 and kdim % tk == 0
    nk = kdim // tk
    j0 = col0 // tn
    in_specs = [pl.BlockSpec((tm, tk), lambda i, j, k: (i, k)),
                pl.BlockSpec((None, tk, tn), lambda i, j, k: (layer, k, j0 + j))]
    in_specs += list(extra_specs)
    out_specs = [pl.BlockSpec((tm, tn), lambda i, j, k: (i, j)) for _ in out_dtypes]
    out_shape = [jax.ShapeDtypeStruct((m, ncols), dt) for dt in out_dtypes]
    scratch = [pltpu.VMEM((tm, tn), F32)] if nk > 1 else []
    return pl.pallas_call(
        functools.partial(_mm_kernel, nk=nk, n_extra=len(extras), n_out=len(out_dtypes),
                          epilogue=epilogue),
        grid=(m // tm, ncols // tn, nk),
        in_specs=in_specs, out_specs=out_specs, out_shape=out_shape,
        scratch_shapes=scratch,
        compiler_params=_params(("parallel", "parallel", "arbitrary")),
        name=name,
    )(a, w, *extras)


def _epi_store(acc, extra, outs):
    for o in outs:
        o[...] = acc.astype(o.dtype)


def _epi_residual(acc, extra, outs):
    outs[0][...] = extra[0][...] + acc


def _epi_headnorm(acc, extra, outs, *, head_dim, scale):
    w = extra[0][...]
    for g in range(acc.shape[1] // head_dim):
        sl = slice(g * head_dim, (g + 1) * head_dim)
        y = _rms(acc[:, sl], w)
        for o in outs:
            o[:, sl] = y if o.dtype == F32 else (y * scale).astype(o.dtype)


def _epi_rotate(acc, extra, outs):
    cos_ref, sin_ref = extra
    for g in range(acc.shape[1] // LANES):
        sl = slice(g * LANES, (g + 1) * LANES)
        x = acc[:, sl]
        lane = lax.broadcasted_iota(jnp.int32, x.shape, 1)
        swapped = jnp.where(lane % 2 == 0, pltpu.roll(x, LANES - 1, axis=1), pltpu.roll(x, 1, axis=1))
        y = x * cos_ref[:, sl] + swapped * sin_ref[:, sl]
        for o in outs:
            o[:, sl] = y.astype(o.dtype)


def _tile_spec(tm, tn):
    return pl.BlockSpec((tm, tn), lambda i, j, k: (i, j))


def _diff_lambda(lq1_ref, lk1_ref, lq2_ref, lk2_ref, lam_init):
    s1 = jnp.sum(lq1_ref[...] * lk1_ref[...], axis=-1, keepdims=True)
    s2 = jnp.sum(lq2_ref[...] * lk2_ref[...], axis=-1, keepdims=True)
    return jnp.exp(s1) - jnp.exp(s2) + lam_init


def _attn_prompt_kernel(q_ref, k_ref, v_ref, lq1_ref, lk1_ref, lq2_ref, lk2_ref, sw_ref, o_ref,
                        *, head_dim, lam_init):
    qi = pl.program_id(2)
    q, k, v = q_ref[0], k_ref[0], v_ref[0]
    tq, s = q.shape[0], k.shape[0]
    lam = _diff_lambda(lq1_ref, lk1_ref, lq2_ref, lk2_ref, lam_init)
    qpos = qi * tq + lax.broadcasted_iota(jnp.int32, (tq, s), 0)
    kpos = lax.broadcasted_iota(jnp.int32, (tq, s), 1)
    causal = kpos <= qpos

    def softmax_map(c):
        sl = slice(c * head_dim, (c + 1) * head_dim)
        sc = jnp.where(causal, _dot_nt(q[:, sl], k[:, sl]), NEG)
        p = jnp.exp(sc - jnp.max(sc, axis=-1, keepdims=True))
        return p / jnp.sum(p, axis=-1, keepdims=True)

    a = softmax_map(0) - lam * softmax_map(1)
    o = _dot(a.astype(BF16), v)
    o_ref[0] = (_rms(o, sw_ref[...]) * (1.0 - lam_init)).astype(o_ref.dtype)


def _attn_prompt(q, k, v, lam_rows, subln_row, *, heads, head_dim, v_dim, lam_init, tq=256):
    b, s, _ = q.shape
    tq = min(tq, s)
    vec = lambda n: pl.BlockSpec((1, n), lambda bi, h, qi: (0, 0))
    return pl.pallas_call(
        functools.partial(_attn_prompt_kernel, head_dim=head_dim, lam_init=lam_init),
        grid=(b, heads, s // tq),
        in_specs=[pl.BlockSpec((1, tq, 2 * head_dim), lambda bi, h, qi: (bi, qi, h)),
                  pl.BlockSpec((1, s, 2 * head_dim), lambda bi, h, qi: (bi, 0, h)),
                  pl.BlockSpec((1, s, v_dim), lambda bi, h, qi: (bi, 0, h)),
                  vec(head_dim), vec(head_dim), vec(head_dim), vec(head_dim), vec(v_dim)],
        out_specs=pl.BlockSpec((1, tq, v_dim), lambda bi, h, qi: (bi, qi, h)),
        out_shape=jax.ShapeDtypeStruct((b, s, heads * v_dim), BF16),
        compiler_params=_params(("parallel", "parallel", "parallel")),
        name="attn_prompt",
    )(q, k, v, *lam_rows, subln_row)


def _attn_sample_kernel(pt_ref, qs_ref, kn_ref, vn_ref, lq1_ref, lk1_ref, lq2_ref, lk2_ref, sw_ref,
                        *rest, pages_per_step, heads, lam_init):
    k_refs = rest[:pages_per_step]
    v_refs = rest[pages_per_step:2 * pages_per_step]
    o_ref, m_scr, l_scr, acc_scr = rest[2 * pages_per_step:]
    step = pl.program_id(1)
    t_new = kn_ref.shape[1]
    rows = 2 * heads * t_new

    def update(k_ref, v_ref, own_tokens):
        keys = k_ref.shape[0]
        cols = keys * heads
        sc = []
        for c in range(2):
            kc = k_ref[:, pl.ds(c, heads, stride=2), :].reshape(cols, k_ref.shape[2])
            sc.append(_dot_nt(qs_ref[0, c], kc.astype(BF16)))
        sc = jnp.concatenate(sc, axis=0)
        r = lax.broadcasted_iota(jnp.int32, (rows, cols), 0)
        col = lax.broadcasted_iota(jnp.int32, (rows, cols), 1)
        valid = (col % heads) == ((r // t_new) % heads)
        if own_tokens:
            valid = valid & ((col // heads) <= (r % t_new))
        sc = jnp.where(valid, sc, NEG)
        m_old = m_scr[...]
        m_new = jnp.maximum(m_old, jnp.max(sc, axis=-1, keepdims=True))
        alpha = jnp.exp(m_old - m_new)
        p = jnp.exp(sc - m_new)
        l_scr[...] = alpha * l_scr[...] + jnp.sum(p, axis=-1, keepdims=True)
        vmat = v_ref[...].reshape(cols, v_ref.shape[2])
        acc_scr[...] = alpha * acc_scr[...] + _dot(p.astype(BF16), vmat.astype(BF16))
        m_scr[...] = m_new

    @pl.when(step == 0)
    def _():
        m_scr[...] = jnp.full(m_scr.shape, NEG, F32)
        l_scr[...] = jnp.zeros(l_scr.shape, F32)
        acc_scr[...] = jnp.zeros(acc_scr.shape, F32)
        update(kn_ref.at[0], vn_ref.at[0], True)

    for g in range(pages_per_step):
        update(k_refs[g], v_refs[g], False)

    @pl.when(step == pl.num_programs(1) - 1)
    def _():
        lam = _diff_lambda(lq1_ref, lk1_ref, lq2_ref, lk2_ref, lam_init)
        o = acc_scr[...] / l_scr[...]
        half = rows // 2
        od = o[:half] - lam * o[half:]
        o_ref[0] = (_rms(od, sw_ref[...]) * (1.0 - lam_init)).astype(o_ref.dtype)


def _attn_sample(qs, k_new, v_new, cache_k, cache_v, layer, page_table, lam_rows, subln_row,
                 *, lam_init, pages_per_step=4):
    b, t_new, sub, hd = k_new.shape
    heads, vd = v_new.shape[2], v_new.shape[3]
    page = cache_k.shape[2]
    n_pages = page_table.shape[1]
    g = pages_per_step
    assert n_pages % g == 0 and heads % SUBLANES == 0
    rows = 2 * heads * t_new
    vec = lambda n: pl.BlockSpec((1, n), lambda bi, s, pt: (0, 0))

    def page_spec(shape, gi):
        return pl.BlockSpec((None, None) + shape, lambda bi, s, pt: (layer, pt[bi, s * g + gi], 0, 0, 0))

    in_specs = [pl.BlockSpec((1, 2, heads * t_new, hd), lambda bi, s, pt: (bi, 0, 0, 0)),
                pl.BlockSpec((1, t_new, sub, hd), lambda bi, s, pt: (bi, 0, 0, 0)),
                pl.BlockSpec((1, t_new, heads, vd), lambda bi, s, pt: (bi, 0, 0, 0)),
                vec(hd), vec(hd), vec(hd), vec(hd), vec(vd)]
    in_specs += [page_spec((page, sub, hd), gi) for gi in range(g)]
    in_specs += [page_spec((page, heads, vd), gi) for gi in range(g)]
    return pl.pallas_call(
        functools.partial(_attn_sample_kernel, pages_per_step=g, heads=heads, lam_init=lam_init),
        grid_spec=pltpu.PrefetchScalarGridSpec(
            num_scalar_prefetch=1,
            grid=(b, n_pages // g),
            in_specs=in_specs,
            out_specs=pl.BlockSpec((1, heads * t_new, vd), lambda bi, s, pt: (bi, 0, 0)),
            scratch_shapes=[pltpu.VMEM((rows, 1), F32), pltpu.VMEM((rows, 1), F32),
                            pltpu.VMEM((rows, vd), F32)]),
        out_shape=jax.ShapeDtypeStruct((b, heads * t_new, vd), BF16),
        compiler_params=_params(("parallel", "arbitrary")),
        name="attn_sample",
    )(page_table, qs, k_new, v_new, *lam_rows, subln_row, *([cache_k] * g), *([cache_v] * g))


def _retention_kernel(lg_ref, gl_ref, q_ref, k_ref, v_ref, g_ref, *rest, length, has_state):
    if has_state:
        r0_ref, nw_ref, y_ref, rout_ref, r_scr = rest
    else:
        nw_ref, y_ref, rout_ref, r_scr = rest
    h, c = pl.program_id(1), pl.program_id(2)

    @pl.when(c == 0)
    def _():
        r_scr[...] = r0_ref[0, 0] if has_state else jnp.zeros(r_scr.shape, F32)

    lg = lg_ref[h]
    q, k, v = q_ref[0], k_ref[0], v_ref[0]
    lp = q.shape[0]
    rel = (lax.broadcasted_iota(jnp.int32, (lp, lp), 0)
           - lax.broadcasted_iota(jnp.int32, (lp, lp), 1)).astype(F32)
    dmask = jnp.where(rel >= 0, jnp.exp(jnp.maximum(rel, 0.0) * lg), 0.0)
    inner = _dot_nt(q, k.astype(BF16)) * dmask
    idx = lax.broadcasted_iota(jnp.int32, (lp, 1), 0).astype(F32)
    r_old = r_scr[...]
    o = _dot(inner.astype(BF16), v) + _dot(q, r_old.astype(BF16)) * jnp.exp((idx + 1.0) * lg)
    kdec = k * jnp.exp((length - 1.0 - idx) * lg)
    r_new = r_old * gl_ref[h] + _dot_tn(kdec.astype(BF16), v)
    r_scr[...] = r_new
    g = g_ref[0]
    y_ref[0] = (g * jax.nn.sigmoid(g) * _rms(o, nw_ref[...])).astype(y_ref.dtype)

    @pl.when(c == pl.num_programs(2) - 1)
    def _():
        rout_ref[0, 0] = r_new


def _retention(q, k, v, g, state, norm_row, log_g, *, heads, chunk, length):
    b, s, _ = q.shape
    dk, dv = q.shape[2] // heads, v.shape[2] // heads
    decay_len = jnp.exp(length * log_g)
    seq = lambda n: pl.BlockSpec((1, chunk, n), lambda bi, h, c, *_: (bi, c, h))
    st = pl.BlockSpec((1, 1, dk, dv), lambda bi, h, c, *_: (bi, h, 0, 0))
    in_specs = [seq(dk), seq(dk), seq(dv), seq(dv)]
    args = [q, k, v, g]
    if state is not None:
        in_specs.append(st)
        args.append(state)
    in_specs.append(pl.BlockSpec((1, dv), lambda bi, h, c, *_: (0, 0)))
    args.append(norm_row)
    return pl.pallas_call(
        functools.partial(_retention_kernel, length=float(length), has_state=state is not None),
        grid_spec=pltpu.PrefetchScalarGridSpec(
            num_scalar_prefetch=2,
            grid=(b, heads, s // chunk),
            in_specs=in_specs,
            out_specs=[seq(dv), st],
            scratch_shapes=[pltpu.VMEM((dk, dv), F32)]),
        out_shape=[jax.ShapeDtypeStruct((b, s, heads * dv), BF16),
                   jax.ShapeDtypeStruct((b, heads, dk, dv), F32)],
        compiler_params=_params(("parallel", "parallel", "arbitrary")),
        name="retention",
    )(log_g, decay_len, *args)


def _ffn_up_kernel(a_ref, wg_ref, wu_ref, cwg_ref, cwu_ref, cbg_ref, cbu_ref, *rest,
                   tiles_per_seq, has_add):
    if has_add:
        addg_ref, addu_ref, act_ref, outg_ref, outu_ref, eg_scr, eu_scr = rest
    else:
        act_ref, outg_ref, outu_ref, eg_scr, eu_scr = rest
        addg_ref = addu_ref = None
    i = pl.program_id(1)
    tm = a_ref.shape[0]
    pad = SUBLANES

    @pl.when(i % tiles_per_seq == 0)
    def _():
        eg_scr[0:pad, :] = jnp.zeros((pad, eg_scr.shape[1]), F32)
        eu_scr[0:pad, :] = jnp.zeros((pad, eu_scr.shape[1]), F32)

    a = a_ref[...]

    def half(w_ref, cw_ref, cb_ref, add_ref, e_scr, out_ref):
        u = _dot(a, w_ref[...].astype(BF16))
        if has_add:
            u = u + add_ref[...]
            out_ref[...] = u
        e_scr[pad:pad + tm, :] = u
        conv = (cb_ref[...] + e_scr[pad - 2:pad - 2 + tm, :] * cw_ref[0:1, :]
                + e_scr[pad - 1:pad - 1 + tm, :] * cw_ref[1:2, :] + u * cw_ref[2:3, :])
        if not has_add:
            out_ref[0] = e_scr[pad + tm - 2:pad + tm, :]
        e_scr[0:pad, :] = e_scr[tm:tm + pad, :]
        return conv

    gate = half(wg_ref, cwg_ref, cbg_ref, addg_ref, eg_scr, outg_ref)
    up = half(wu_ref, cwu_ref, cbu_ref, addu_ref, eu_scr, outu_ref)
    act_ref[...] = (gate * jax.nn.sigmoid(gate) * up).astype(act_ref.dtype)


def _ffn_up(a, w_up, conv_w, conv_b, layer, *, d_ff, seq_len, add=None, tm=1024, tn=256):
    m, kdim = a.shape
    tm, tn = min(tm, m, seq_len), min(tn, d_ff)
    assert m % tm == 0 and d_ff % tn == 0 and seq_len % tm == 0
    nf = d_ff // tn
    has_add = add is not None
    wspec = lambda off: pl.BlockSpec((None, kdim, tn), lambda j, i: (layer, 0, off + j))
    cwspec = lambda off: pl.BlockSpec((conv_w.shape[0], tn), lambda j, i: (0, off + j))
    cbspec = lambda off: pl.BlockSpec((1, tn), lambda j, i: (0, off + j))
    in_specs = [pl.BlockSpec((tm, kdim), lambda j, i: (i, 0)), wspec(0), wspec(nf),
                cwspec(0), cwspec(nf), cbspec(0), cbspec(nf)]
    args = [a, w_up, w_up, conv_w, conv_w, conv_b, conv_b]
    act_spec = pl.BlockSpec((tm, tn), lambda j, i: (i, j))
    if has_add:
        assert m == tm
        in_specs += [pl.BlockSpec((tm, tn), lambda j, i: (i, j)),
                     pl.BlockSpec((tm, tn), lambda j, i: (i, nf + j))]
        args += [add, add]
        extra_specs = [act_spec, act_spec]
        extra_shape = [jax.ShapeDtypeStruct((m, d_ff), F32)] * 2
    else:
        tps = seq_len // tm
        st_spec = pl.BlockSpec((1, 2, tn), lambda j, i: (i // tps, 0, j))
        extra_specs = [st_spec, st_spec]
        extra_shape = [jax.ShapeDtypeStruct((m // seq_len, 2, d_ff), F32)] * 2
    return pl.pallas_call(
        functools.partial(_ffn_up_kernel, tiles_per_seq=seq_len // tm, has_add=has_add),
        grid=(nf, m // tm),
        in_specs=in_specs,
        out_specs=[act_spec] + extra_specs,
        out_shape=[jax.ShapeDtypeStruct((m, d_ff), BF16)] + extra_shape,
        scratch_shapes=[pltpu.VMEM((tm + SUBLANES, tn), F32), pltpu.VMEM((tm + SUBLANES, tn), F32)],
        compiler_params=_params(("parallel", "arbitrary")),
        name="ffn_up",
    )(*args)


def _lambda_init(layer_idx):
    return 0.8 - 0.6 * math.exp(-0.3 * layer_idx)


def _rotation_tables(pos, dk, scale):
    inv = 1.0 / (RET_ROT_BASE ** jnp.linspace(0.0, 1.0, dk // 2, dtype=F32))
    ang = pos.astype(F32)[:, None] * inv[None, :]
    cos, sin = jnp.cos(ang), jnp.sin(ang)
    cos_full = jnp.stack([cos, cos], axis=-1).reshape(pos.shape[0], dk)
    sin_signed = jnp.stack([-sin, sin], axis=-1).reshape(pos.shape[0], dk)
    return cos_full * scale, sin_signed * scale


def _ffn(x, h_rows, layer, w_up, conv_w, conv_b, w_down, *, seq_len, add=None):
    d_ff = w_down.shape[1]
    act, out_g, out_u = _ffn_up(h_rows, w_up, conv_w[layer], conv_b[layer][None, :], layer,
                                d_ff=d_ff, seq_len=seq_len, add=add)
    tk = d_ff // 2 if d_ff > 4096 else d_ff
    tn = 256 if d_ff > 4096 else 512
    (y,) = _matmul(act, w_down, layer, col0=0, ncols=x.shape[1], out_dtypes=[F32],
                   epilogue=_epi_residual, extras=[x], extra_specs=[_tile_spec(min(1024, x.shape[0]), tn)],
                   tn=tn, tk=tk, name="ffn_down")
    return y, out_g, out_u


def kernel(x_prompt, x_sample, cache_k, cache_v, state_ret, state_ffn, page_table, norm_mix_w, norm_ffn_w, w_attn_qkv, q_norm_w, k_norm_w, lambda_q1, lambda_k1, lambda_q2, lambda_k2, subln_w, w_attn_o, w_ret_qkvg, ret_norm_w, w_ret_o, w_ffn_up, ffn_conv_w, ffn_conv_b, w_ffn_down):
    bp, sp, d = x_prompt.shape
    bs, ss, _ = x_sample.shape
    depth = norm_mix_w.shape[0]
    hd, vd = q_norm_w.shape[-1], subln_w.shape[-1]
    sub, heads = cache_k.shape[3], cache_v.shape[3]
    qw, vw = sub * hd, heads * vd
    past = page_table.shape[1] * cache_k.shape[2]
    rheads, dk, dv = state_ret.shape[2], state_ret.shape[3], state_ret.shape[4]
    rqk, rv = rheads * dk, rheads * dv
    d_ff = w_ffn_down.shape[1]
    conv_taps = ffn_conv_w.shape[1] - 1
    row = lambda w: w[None, :]

    xp = x_prompt.reshape(bp * sp, d)
    xs = x_sample.reshape(bs * ss, d)
    kp_l, vp_l, ks_l, vs_l, rp_l, rs_l, fp_l, fs_l = [], [], [], [], [], [], [], []
    for i in range(depth):
        hp = _rmsnorm(xp, row(norm_mix_w[i]))
        hs = _rmsnorm(xs, row(norm_mix_w[i]))
        if i % 2 == 0:
            a = i // 2
            lam_init = _lambda_init(i)
            lam_rows = [row(lambda_q1[a]), row(lambda_k1[a]), row(lambda_q2[a]), row(lambda_k2[a])]
            qnorm = functools.partial(_epi_headnorm, head_dim=hd, scale=hd ** -0.5)
            knorm = functools.partial(_epi_headnorm, head_dim=hd, scale=1.0)
            nspec = [pl.BlockSpec((1, hd), lambda i_, j, k: (0, 0))]

            def qkv(h_rows):
                (q,) = _matmul(h_rows, w_attn_qkv, a, col0=0, ncols=qw, out_dtypes=[BF16], epilogue=qnorm,
                               extras=[row(q_norm_w[a])], extra_specs=nspec, name="attn_q")
                k32, k16 = _matmul(h_rows, w_attn_qkv, a, col0=qw, ncols=qw, out_dtypes=[F32, BF16],
                                   epilogue=knorm, extras=[row(k_norm_w[a])], extra_specs=nspec,
                                   name="attn_k")
                v32, v16 = _matmul(h_rows, w_attn_qkv, a, col0=2 * qw, ncols=vw, out_dtypes=[F32, BF16],
                                   epilogue=_epi_store, name="attn_v")
                return q, k32, k16, v32, v16

            def out_proj(x, o_rows):
                (y,) = _matmul(o_rows, w_attn_o, a, col0=0, ncols=d, out_dtypes=[F32],
                               epilogue=_epi_residual, extras=[x],
                               extra_specs=[_tile_spec(min(1024, x.shape[0]), 512)], name="attn_o")
                return y

            q, k32, k16, v32, v16 = qkv(hp)
            op = _attn_prompt(q.reshape(bp, sp, qw), k16.reshape(bp, sp, qw), v16.reshape(bp, sp, vw),
                              lam_rows, row(subln_w[a]), heads=heads, head_dim=hd, v_dim=vd,
                              lam_init=lam_init)
            xp = out_proj(xp, op.reshape(bp * sp, vw))
            kp_l.append(k32.reshape(bp, sp, sub, hd))
            vp_l.append(v32.reshape(bp, sp, heads, vd))

            q, k32, _, v32, _ = qkv(hs)
            k_new = k32.reshape(bs, ss, sub, hd)
            v_new = v32.reshape(bs, ss, heads, vd)
            qs = q.reshape(bs, ss, heads, 2, hd).transpose(0, 3, 2, 1, 4).reshape(bs, 2, heads * ss, hd)
            osm = _attn_sample(qs, k_new, v_new, cache_k, cache_v, a, page_table, lam_rows,
                               row(subln_w[a]), lam_init=lam_init)
            osm = osm.reshape(bs, heads, ss, vd).transpose(0, 2, 1, 3).reshape(bs * ss, vw)
            xs = out_proj(xs, osm)
            ks_l.append(k_new)
            vs_l.append(v_new)
        else:
            r = i // 2
            log_g = jnp.log1p(-jnp.exp2(-5.0 - jnp.arange(rheads, dtype=F32)))

            def project(h_rows, cos_q, sin_q, cos_k, sin_k, tm):
                reps = 512 // dk
                tabs = [jnp.tile(t, (1, reps)) for t in (cos_q, sin_q, cos_k, sin_k)]
                nrep = tabs[0].shape[0] // tm
                tspec = [pl.BlockSpec((tm, 512), lambda i_, j, k: (i_ % nrep, 0))] * 2
                (q,) = _matmul(h_rows, w_ret_qkvg, r, col0=0, ncols=rqk, out_dtypes=[BF16],
                               epilogue=_epi_rotate, extras=tabs[:2], extra_specs=tspec, tm=tm, name="ret_q")
                (k,) = _matmul(h_rows, w_ret_qkvg, r, col0=rqk, ncols=rqk, out_dtypes=[F32],
                               epilogue=_epi_rotate, extras=tabs[2:], extra_specs=tspec, tm=tm, name="ret_k")
                (v,) = _matmul(h_rows, w_ret_qkvg, r, col0=2 * rqk, ncols=rv, out_dtypes=[BF16],
                               epilogue=_epi_store, tm=tm, name="ret_v")
                (g,) = _matmul(h_rows, w_ret_qkvg, r, col0=2 * rqk + rv, ncols=rv, out_dtypes=[F32],
                               epilogue=_epi_store, tm=tm, name="ret_g")
                return q, k, v, g

            def out_proj(x, y_rows):
                (y,) = _matmul(y_rows, w_ret_o, r, col0=0, ncols=d, out_dtypes=[F32],
                               epilogue=_epi_residual, extras=[x],
                               extra_specs=[_tile_spec(min(1024, x.shape[0]), 512)], name="ret_o")
                return y

            tm = min(1024, sp)
            cq, sq = _rotation_tables(jnp.arange(sp), dk, 1.0)
            ck, sk = _rotation_tables(jnp.arange(sp), dk, dk ** -0.5)
            q, k, v, g = project(hp, cq, sq, ck, sk, tm)
            chunk = min(128, sp)
            y, rp = _retention(q.reshape(bp, sp, rqk), k.reshape(bp, sp, rqk), v.reshape(bp, sp, rv),
                               g.reshape(bp, sp, rv), None, row(ret_norm_w[r]), log_g,
                               heads=rheads, chunk=chunk, length=chunk)
            xp = out_proj(xp, y.reshape(bp * sp, rv))
            rp_l.append(rp)

            pos_s = past + jnp.arange(ss)
            cq, sq = (jnp.tile(t, (bs, 1)) for t in _rotation_tables(pos_s, dk, 1.0))
            ck, sk = (jnp.tile(t, (bs, 1)) for t in _rotation_tables(pos_s, dk, dk ** -0.5))
            q, k, v, g = project(hs, cq, sq, ck, sk, bs * ss)
            chunk = 128
            padded = lambda t: jnp.pad(t.reshape(bs, ss, -1), ((0, 0), (0, chunk - ss), (0, 0)))
            y, rs = _retention(padded(q), padded(k), padded(v), padded(g), state_ret[r],
                               row(ret_norm_w[r]), log_g, heads=rheads, chunk=chunk, length=ss)
            xs = out_proj(xs, y[:, :ss].reshape(bs * ss, rv))
            rs_l.append(rs)

        hp = _rmsnorm(xp, row(norm_ffn_w[i]))
        xp, st_g, st_u = _ffn(xp, hp, i, w_ffn_up, ffn_conv_w, ffn_conv_b, w_ffn_down, seq_len=sp)
        fp_l.append(jnp.concatenate([st_g, st_u], axis=-1))

        group = 2 * SUBLANES
        assert conv_taps + ss <= group
        hs = _rmsnorm(xs, row(norm_ffn_w[i]))
        hs_pad = jnp.pad(hs.reshape(bs, ss, d), ((0, 0), (conv_taps, group - conv_taps - ss), (0, 0)))
        add = jnp.pad(state_ffn[i], ((0, 0), (0, group - conv_taps), (0, 0)))
        xs_pad = jnp.pad(xs.reshape(bs, ss, d), ((0, 0), (conv_taps, group - conv_taps - ss), (0, 0)))
        ys_pad, ext_g, ext_u = _ffn(xs_pad.reshape(bs * group, d), hs_pad.reshape(bs * group, d), i,
                                    w_ffn_up, ffn_conv_w, ffn_conv_b, w_ffn_down,
                                    seq_len=bs * group, add=add.reshape(bs * group, 2 * d_ff))
        xs = ys_pad.reshape(bs, group, d)[:, conv_taps:conv_taps + ss].reshape(bs * ss, d)
        ext = jnp.concatenate([ext_g, ext_u], axis=-1).reshape(bs, group, 2 * d_ff)
        fs_l.append(ext[:, ss:ss + conv_taps])

    return (xp.reshape(bp, sp, d), xs.reshape(bs, ss, d),
            jnp.stack(kp_l), jnp.stack(vp_l), jnp.stack(ks_l), jnp.stack(vs_l),
            jnp.stack(rp_l), jnp.stack(rs_l), jnp.stack(fp_l), jnp.stack(fs_l))
```

```python
import functools
import math

import jax
import jax.numpy as jnp
from jax import lax
from jax.experimental import pallas as pl
from jax.experimental.pallas import tpu as pltpu

F32 = jnp.float32
BF16 = jnp.bfloat16
NORM_EPS = 1e-6
RET_ROT_BASE = 10000.0
LANES = 128
SUBLANES = 8
BF16_ROWS = 16
VMEM_LIMIT = 56 * 1024 * 1024
NEG = -1e30
ROW_CHUNK = 256
LOG2E = math.log2(math.e)


def _dot(a, b):
    return jnp.dot(a, b, preferred_element_type=F32)


def _dot_nt(a, b):
    return lax.dot_general(a, b, (((1,), (1,)), ((), ())), preferred_element_type=F32)


def _dot_tn(a, b):
    return lax.dot_general(a, b, (((0,), (0,)), ((), ())), preferred_element_type=F32)


def _params(sem):
    return pltpu.CompilerParams(dimension_semantics=sem, vmem_limit_bytes=VMEM_LIMIT)


def _rms(x, w):
    return x * lax.rsqrt(jnp.mean(x * x, axis=-1, keepdims=True) + NORM_EPS) * w


def _as_bf16(x):
    return x if x.dtype == BF16 else x.astype(BF16)


def _rmsnorm_kernel(x_ref, w_ref, o_ref):
    o_ref[...] = _rms(x_ref[...], w_ref[...]).astype(o_ref.dtype)


def _rmsnorm(x, w_row, tr=256):
    rows, d = x.shape
    tr = min(tr, rows)
    return pl.pallas_call(
        _rmsnorm_kernel,
        grid=(rows // tr,),
        in_specs=[pl.BlockSpec((tr, d), lambda i: (i, 0)),
                  pl.BlockSpec((1, d), lambda i: (0, 0))],
        out_specs=pl.BlockSpec((tr, d), lambda i: (i, 0)),
        out_shape=jax.ShapeDtypeStruct((rows, d), BF16),
        compiler_params=_params(("parallel",)),
        name="rmsnorm",
    )(x, w_row)


def _cast_kernel(x_ref, o_ref):
    o_ref[...] = x_ref[...].astype(o_ref.dtype)


def _to_bf16(w):
    nl, kdim, n = w.shape
    tk = max(t for t in range(BF16_ROWS, 1025, BF16_ROWS) if kdim % t == 0)
    return pl.pallas_call(
        _cast_kernel,
        grid=(nl, kdim // tk),
        in_specs=[pl.BlockSpec((1, tk, n), lambda l, i: (l, i, 0))],
        out_specs=pl.BlockSpec((1, tk, n), lambda l, i: (l, i, 0)),
        out_shape=jax.ShapeDtypeStruct(w.shape, BF16),
        compiler_params=_params(("parallel", "parallel")),
        name="cast_bf16",
    )(w)


def _mm_kernel(*refs, nk, n_extra, n_out, epilogue):
    a_ref, w_ref = refs[:2]
    extra = refs[2:2 + n_extra]
    outs = refs[2 + n_extra:2 + n_extra + n_out]
    w = _as_bf16(w_ref[...])
    tm = a_ref.shape[0]
    cm = min(ROW_CHUNK, tm)
    chunks = [pl.ds(c * cm, cm) for c in range(tm // cm)]
    if nk == 1:
        for rows in chunks:
            epilogue(_dot(a_ref[rows, :], w), extra, outs, rows)
        return
    acc_ref = refs[-1]
    k = pl.program_id(2)

    @pl.when(k == 0)
    def _():
        acc_ref[...] = jnp.zeros(acc_ref.shape, F32)

    for rows in chunks:
        acc_ref[rows, :] += _dot(a_ref[rows, :], w)

    @pl.when(k == nk - 1)
    def _():
        for rows in chunks:
            epilogue(acc_ref[rows, :], extra, outs, rows)


def _matmul(a, w, layer, *, col0, ncols, out_dtypes, epilogue, extras=(), extra_specs=(),
            tm=1024, tn=512, tk=4096, name="matmul"):
    m, kdim = a.shape
    tm, tn, tk = min(tm, m), min(tn, ncols), min(tk, kdim)
    assert m % tm == 0 and ncols % tn == 0 and col0 % tn == 0 and kdim % tk == 0
    nk = kdim // tk
    j0 = col0 // tn
    in_specs = [pl.BlockSpec((tm, tk), lambda i, j, k: (i, k)),
                pl.BlockSpec((None, tk, tn), lambda i, j, k: (layer, k, j0 + j))]
    in_specs += list(extra_specs)
    out_specs = [pl.BlockSpec((tm, tn), lambda i, j, k: (i, j)) for _ in out_dtypes]
    out_shape = [jax.ShapeDtypeStruct((m, ncols), dt) for dt in out_dtypes]
    scratch = [pltpu.VMEM((tm, tn), F32)] if nk > 1 else []
    return pl.pallas_call(
        functools.partial(_mm_kernel, nk=nk, n_extra=len(extras), n_out=len(out_dtypes),
                          epilogue=epilogue),
        grid=(m // tm, ncols // tn, nk),
        in_specs=in_specs, out_specs=out_specs, out_shape=out_shape,
        scratch_shapes=scratch,
        compiler_params=_params(("parallel", "parallel", "arbitrary")),
        name=name,
    )(a, w, *extras)


def _epi_store(acc, extra, outs, rows):
    for o in outs:
        o[rows, :] = acc.astype(o.dtype)


def _epi_residual(acc, extra, outs, rows):
    outs[0][rows, :] = extra[0][rows, :] + acc


def _epi_headnorm(acc, extra, outs, rows, *, head_dim, scale):
    w = extra[0][...]
    for g in range(acc.shape[1] // head_dim):
        sl = slice(g * head_dim, (g + 1) * head_dim)
        y = _rms(acc[:, sl], w)
        for o in outs:
            o[rows, sl] = y if o.dtype == F32 else (y * scale).astype(o.dtype)


def _epi_rotate(acc, extra, outs, rows):
    cos_ref, sin_ref = extra
    for g in range(acc.shape[1] // LANES):
        sl = slice(g * LANES, (g + 1) * LANES)
        x = acc[:, sl]
        lane = lax.broadcasted_iota(jnp.int32, x.shape, 1)
        swapped = jnp.where(lane % 2 == 0, pltpu.roll(x, LANES - 1, axis=1), pltpu.roll(x, 1, axis=1))
        y = x * cos_ref[rows, sl] + swapped * sin_ref[rows, sl]
        for o in outs:
            o[rows, sl] = y.astype(o.dtype)


def _tile_spec(tm, tn):
    return pl.BlockSpec((tm, tn), lambda i, j, k: (i, j))


def _diff_lambda(lq1_ref, lk1_ref, lq2_ref, lk2_ref, lam_init):
    s1 = jnp.sum(lq1_ref[...] * lk1_ref[...], axis=-1, keepdims=True)
    s2 = jnp.sum(lq2_ref[...] * lk2_ref[...], axis=-1, keepdims=True)
    return jnp.exp(s1) - jnp.exp(s2) + lam_init


def _attn_prompt_kernel(q_ref, k_ref, v_ref, lq1_ref, lk1_ref, lq2_ref, lk2_ref, sw_ref, o_ref,
                        *, head_dim, lam_init, cq):
    qi = pl.program_id(2)
    tq, s = q_ref.shape[1], k_ref.shape[1]
    n_chunks = tq // cq
    lam = _diff_lambda(lq1_ref, lk1_ref, lq2_ref, lk2_ref, lam_init)
    sw = sw_ref[...]
    tri = (lax.broadcasted_iota(jnp.int32, (cq, cq), 1) <= lax.broadcasted_iota(jnp.int32, (cq, cq), 0))

    def chunk(g):
        r0 = (g % n_chunks) * cq
        pre = g * cq
        q = q_ref[0, r0:r0 + cq, :]
        weights, sums = [], []
        for c in range(2):
            sl = slice(c * head_dim, (c + 1) * head_dim)
            sd = jnp.where(tri, _dot_nt(q[:, sl], k_ref[0, pre:pre + cq, sl]), NEG)
            m = jnp.max(sd, axis=-1, keepdims=True)
            if pre:
                sp = _dot_nt(q[:, sl], k_ref[0, 0:pre, sl])
                m = jnp.maximum(m, jnp.max(sp, axis=-1, keepdims=True))
            pd = jnp.exp2(sd - m)
            l = jnp.sum(pd, axis=-1, keepdims=True)
            pp = None
            if pre:
                pp = jnp.exp2(sp - m)
                l = l + jnp.sum(pp, axis=-1, keepdims=True)
            weights.append((pp, pd))
            sums.append(l)
        r1 = 1.0 / sums[0]
        r2 = lam / sums[1]
        o = _dot((weights[0][1] * r1 - weights[1][1] * r2).astype(BF16), v_ref[0, pre:pre + cq, :])
        if pre:
            o = o + _dot((weights[0][0] * r1 - weights[1][0] * r2).astype(BF16), v_ref[0, 0:pre, :])
        o_ref[0, r0:r0 + cq, :] = (_rms(o, sw) * (1.0 - lam_init)).astype(o_ref.dtype)

    for n in range(s // tq):
        @pl.when(qi == n)
        def _(n=n):
            for rc in range(n_chunks):
                chunk(n * n_chunks + rc)


def _attn_prompt(q, k, v, lam_rows, subln_row, *, heads, head_dim, v_dim, lam_init, tq=1024, cq=256):
    b, s, _ = q.shape
    tq = min(tq, s)
    cq = min(cq, tq)
    assert s % tq == 0 and tq % cq == 0
    vec = lambda n: pl.BlockSpec((1, n), lambda bi, h, qi: (0, 0))
    return pl.pallas_call(
        functools.partial(_attn_prompt_kernel, head_dim=head_dim, lam_init=lam_init, cq=cq),
        grid=(b, heads, s // tq),
        in_specs=[pl.BlockSpec((1, tq, 2 * head_dim), lambda bi, h, qi: (bi, qi, h)),
                  pl.BlockSpec((1, s, 2 * head_dim), lambda bi, h, qi: (bi, 0, h)),
                  pl.BlockSpec((1, s, v_dim), lambda bi, h, qi: (bi, 0, h)),
                  vec(head_dim), vec(head_dim), vec(head_dim), vec(head_dim), vec(v_dim)],
        out_specs=pl.BlockSpec((1, tq, v_dim), lambda bi, h, qi: (bi, qi, h)),
        out_shape=jax.ShapeDtypeStruct((b, s, heads * v_dim), BF16),
        compiler_params=_params(("parallel", "parallel", "parallel")),
        name="attn_prompt",
    )(q, k, v, *lam_rows, subln_row)


def _attn_sample_kernel(pt_ref, qs_ref, kn_ref, vn_ref, lq1_ref, lk1_ref, lq2_ref, lk2_ref, sw_ref,
                        *rest, pages_per_step, heads, lam_init):
    k_refs = rest[:pages_per_step]
    v_refs = rest[pages_per_step:2 * pages_per_step]
    o_ref, m_scr, l_scr, acc_scr = rest[2 * pages_per_step:]
    step = pl.program_id(1)
    t_new = kn_ref.shape[1]
    rows = 2 * heads * t_new

    def update(k_ref, v_ref, own_tokens):
        keys = k_ref.shape[0]
        cols = keys * heads
        sc = []
        for c in range(2):
            kc = k_ref[:, pl.ds(c, heads, stride=2), :].reshape(cols, k_ref.shape[2])
            sc.append(_dot_nt(qs_ref[0, c], kc.astype(BF16)))
        sc = jnp.concatenate(sc, axis=0)
        r = lax.broadcasted_iota(jnp.int32, (rows, cols), 0)
        col = lax.broadcasted_iota(jnp.int32, (rows, cols), 1)
        valid = (col % heads) == ((r // t_new) % heads)
        if own_tokens:
            valid = valid & ((col // heads) <= (r % t_new))
        sc = jnp.where(valid, sc, NEG)
        m_old = m_scr[...]
        m_new = jnp.maximum(m_old, jnp.max(sc, axis=-1, keepdims=True))
        alpha = jnp.exp(m_old - m_new)
        p = jnp.exp(sc - m_new)
        l_scr[...] = alpha * l_scr[...] + jnp.sum(p, axis=-1, keepdims=True)
        vmat = v_ref[...].reshape(cols, v_ref.shape[2])
        acc_scr[...] = alpha * acc_scr[...] + _dot(p.astype(BF16), vmat.astype(BF16))
        m_scr[...] = m_new

    @pl.when(step == 0)
    def _():
        m_scr[...] = jnp.full(m_scr.shape, NEG, F32)
        l_scr[...] = jnp.zeros(l_scr.shape, F32)
        acc_scr[...] = jnp.zeros(acc_scr.shape, F32)
        update(kn_ref.at[0], vn_ref.at[0], True)

    for g in range(pages_per_step):
        update(k_refs[g], v_refs[g], False)

    @pl.when(step == pl.num_programs(1) - 1)
    def _():
        lam = _diff_lambda(lq1_ref, lk1_ref, lq2_ref, lk2_ref, lam_init)
        o = acc_scr[...] / l_scr[...]
        half = rows // 2
        od = o[:half] - lam * o[half:]
        o_ref[0] = (_rms(od, sw_ref[...]) * (1.0 - lam_init)).astype(o_ref.dtype)


def _attn_sample(qs, k_new, v_new, cache_k, cache_v, layer, page_table, lam_rows, subln_row,
                 *, lam_init, pages_per_step=4):
    b, t_new, sub, hd = k_new.shape
    heads, vd = v_new.shape[2], v_new.shape[3]
    page = cache_k.shape[2]
    n_pages = page_table.shape[1]
    g = pages_per_step
    assert n_pages % g == 0 and heads % SUBLANES == 0
    rows = 2 * heads * t_new
    vec = lambda n: pl.BlockSpec((1, n), lambda bi, s, pt: (0, 0))

    def page_spec(shape, gi):
        return pl.BlockSpec((None, None) + shape, lambda bi, s, pt: (layer, pt[bi, s * g + gi], 0, 0, 0))

    in_specs = [pl.BlockSpec((1, 2, heads * t_new, hd), lambda bi, s, pt: (bi, 0, 0, 0)),
                pl.BlockSpec((1, t_new, sub, hd), lambda bi, s, pt: (bi, 0, 0, 0)),
                pl.BlockSpec((1, t_new, heads, vd), lambda bi, s, pt: (bi, 0, 0, 0)),
                vec(hd), vec(hd), vec(hd), vec(hd), vec(vd)]
    in_specs += [page_spec((page, sub, hd), gi) for gi in range(g)]
    in_specs += [page_spec((page, heads, vd), gi) for gi in range(g)]
    return pl.pallas_call(
        functools.partial(_attn_sample_kernel, pages_per_step=g, heads=heads, lam_init=lam_init),
        grid_spec=pltpu.PrefetchScalarGridSpec(
            num_scalar_prefetch=1,
            grid=(b, n_pages // g),
            in_specs=in_specs,
            out_specs=pl.BlockSpec((1, heads * t_new, vd), lambda bi, s, pt: (bi, 0, 0)),
            scratch_shapes=[pltpu.VMEM((rows, 1), F32), pltpu.VMEM((rows, 1), F32),
                            pltpu.VMEM((rows, vd), F32)]),
        out_shape=jax.ShapeDtypeStruct((b, heads * t_new, vd), BF16),
        compiler_params=_params(("parallel", "arbitrary")),
        name="attn_sample",
    )(page_table, qs, k_new, v_new, *lam_rows, subln_row, *([cache_k] * g), *([cache_v] * g))


def _retention_kernel(lg_ref, gl_ref, q_ref, k_ref, v_ref, g_ref, *rest, length, has_state, hps):
    if has_state:
        r0_ref, nw_ref, y_ref, rout_ref, r_scr = rest
    else:
        nw_ref, y_ref, rout_ref, r_scr = rest
    hg, c = pl.program_id(1), pl.program_id(2)

    @pl.when(c == 0)
    def _():
        r_scr[...] = r0_ref[0] if has_state else jnp.zeros(r_scr.shape, F32)

    lp = q_ref.shape[1]
    dk, dv = r_scr.shape[1], r_scr.shape[2]
    rel = (lax.broadcasted_iota(jnp.int32, (lp, lp), 0)
           - lax.broadcasted_iota(jnp.int32, (lp, lp), 1)).astype(F32)
    idx = lax.broadcasted_iota(jnp.int32, (lp, 1), 0).astype(F32)
    nw = nw_ref[...]
    for hh in range(hps):
        h = hg * hps + hh
        lg = lg_ref[h]
        ks, vs = slice(hh * dk, (hh + 1) * dk), slice(hh * dv, (hh + 1) * dv)
        q, k, v = q_ref[0, :, ks], k_ref[0, :, ks], v_ref[0, :, vs]
        dmask = jnp.where(rel >= 0, jnp.exp(jnp.maximum(rel, 0.0) * lg), 0.0)
        inner = _dot_nt(q, k.astype(BF16)) * dmask
        r_old = r_scr[hh]
        o = _dot(inner.astype(BF16), v) + _dot(q, r_old.astype(BF16)) * jnp.exp((idx + 1.0) * lg)
        kdec = k * jnp.exp((length - 1.0 - idx) * lg)
        r_scr[hh] = r_old * gl_ref[h] + _dot_tn(kdec.astype(BF16), v)
        g = g_ref[0, :, vs]
        y_ref[0, :, vs] = (g * jax.nn.sigmoid(g) * _rms(o, nw)).astype(y_ref.dtype)

    @pl.when(c == pl.num_programs(2) - 1)
    def _():
        rout_ref[0] = r_scr[...]


def _retention(q, k, v, g, state, norm_row, log_g, *, heads, chunk, length, hps=4):
    b, s, _ = q.shape
    dk, dv = q.shape[2] // heads, v.shape[2] // heads
    hps = min(hps, heads)
    assert heads % hps == 0
    decay_len = jnp.exp(length * log_g)
    seq = lambda n: pl.BlockSpec((1, chunk, hps * n), lambda bi, h, c, *_: (bi, c, h))
    st = pl.BlockSpec((1, hps, dk, dv), lambda bi, h, c, *_: (bi, h, 0, 0))
    in_specs = [seq(dk), seq(dk), seq(dv), seq(dv)]
    args = [q, k, v, g]
    if state is not None:
        in_specs.append(st)
        args.append(state)
    in_specs.append(pl.BlockSpec((1, dv), lambda bi, h, c, *_: (0, 0)))
    args.append(norm_row)
    return pl.pallas_call(
        functools.partial(_retention_kernel, length=float(length), has_state=state is not None, hps=hps),
        grid_spec=pltpu.PrefetchScalarGridSpec(
            num_scalar_prefetch=2,
            grid=(b, heads // hps, s // chunk),
            in_specs=in_specs,
            out_specs=[seq(dv), st],
            scratch_shapes=[pltpu.VMEM((hps, dk, dv), F32)]),
        out_shape=[jax.ShapeDtypeStruct((b, s, heads * dv), BF16),
                   jax.ShapeDtypeStruct((b, heads, dk, dv), F32)],
        compiler_params=_params(("parallel", "parallel", "arbitrary")),
        name="retention",
    )(log_g, decay_len, *args)


def _ffn_up_kernel(a_ref, wg_ref, wu_ref, cwg_ref, cwu_ref, cbg_ref, cbu_ref, *rest,
                   tiles_per_seq, has_add):
    if has_add:
        addg_ref, addu_ref, act_ref, outg_ref, outu_ref, eg_scr, eu_scr, wg_scr, wu_scr = rest
    else:
        act_ref, outg_ref, outu_ref, eg_scr, eu_scr, wg_scr, wu_scr = rest
        addg_ref = addu_ref = None
    i = pl.program_id(1)
    tm = a_ref.shape[0]
    cm = min(ROW_CHUNK, tm)
    pad = SUBLANES

    @pl.when(i == 0)
    def _():
        wg_scr[...] = wg_ref[...].astype(BF16)
        wu_scr[...] = wu_ref[...].astype(BF16)

    @pl.when(i % tiles_per_seq == 0)
    def _():
        eg_scr[0:pad, :] = jnp.zeros((pad, eg_scr.shape[1]), F32)
        eu_scr[0:pad, :] = jnp.zeros((pad, eu_scr.shape[1]), F32)

    def project(c):
        a = a_ref[c * cm:(c + 1) * cm, :]
        return _dot(a, wg_scr[...]), _dot(a, wu_scr[...])

    def conv(u, r0, cw_ref, cb_ref, add_ref, e_scr, out_ref):
        if has_add:
            u = u + add_ref[r0:r0 + cm, :]
            out_ref[r0:r0 + cm, :] = u
        e_scr[pad + r0:pad + r0 + cm, :] = u
        return (cb_ref[...] + e_scr[pad + r0 - 2:pad + r0 - 2 + cm, :] * cw_ref[0:1, :]
                + e_scr[pad + r0 - 1:pad + r0 - 1 + cm, :] * cw_ref[1:2, :] + u * cw_ref[2:3, :])

    n_chunks = tm // cm
    nxt = project(0)
    for c in range(n_chunks):
        ug, uu = nxt
        if c + 1 < n_chunks:
            nxt = project(c + 1)
        r0 = c * cm
        gate = conv(ug, r0, cwg_ref, cbg_ref, addg_ref, eg_scr, outg_ref)
        up = conv(uu, r0, cwu_ref, cbu_ref, addu_ref, eu_scr, outu_ref)
        act_ref[r0:r0 + cm, :] = (gate * jax.nn.sigmoid(gate) * up).astype(act_ref.dtype)

    for e_scr, out_ref in ((eg_scr, outg_ref), (eu_scr, outu_ref)):
        if not has_add:
            out_ref[0] = e_scr[pad + tm - 2:pad + tm, :]
        e_scr[0:pad, :] = e_scr[tm:tm + pad, :]


def _ffn_up(a, w_up, conv_w, conv_b, layer, *, d_ff, seq_len, add=None, tm=1024, tn=256):
    m, kdim = a.shape
    tm, tn = min(tm, m, seq_len), min(tn, d_ff)
    assert m % tm == 0 and d_ff % tn == 0 and seq_len % tm == 0
    nf = d_ff // tn
    has_add = add is not None
    wspec = lambda off: pl.BlockSpec((None, kdim, tn), lambda j, i: (layer, 0, off + j))
    cwspec = lambda off: pl.BlockSpec((conv_w.shape[0], tn), lambda j, i: (0, off + j))
    cbspec = lambda off: pl.BlockSpec((1, tn), lambda j, i: (0, off + j))
    in_specs = [pl.BlockSpec((tm, kdim), lambda j, i: (i, 0)), wspec(0), wspec(nf),
                cwspec(0), cwspec(nf), cbspec(0), cbspec(nf)]
    args = [a, w_up, w_up, conv_w, conv_w, conv_b, conv_b]
    act_spec = pl.BlockSpec((tm, tn), lambda j, i: (i, j))
    if has_add:
        assert m == tm
        in_specs += [pl.BlockSpec((tm, tn), lambda j, i: (i, j)),
                     pl.BlockSpec((tm, tn), lambda j, i: (i, nf + j))]
        args += [add, add]
        extra_specs = [act_spec, act_spec]
        extra_shape = [jax.ShapeDtypeStruct((m, d_ff), F32)] * 2
    else:
        tps = seq_len // tm
        st_spec = pl.BlockSpec((1, 2, tn), lambda j, i: (i // tps, 0, j))
        extra_specs = [st_spec, st_spec]
        extra_shape = [jax.ShapeDtypeStruct((m // seq_len, 2, d_ff), F32)] * 2
    return pl.pallas_call(
        functools.partial(_ffn_up_kernel, tiles_per_seq=seq_len // tm, has_add=has_add),
        grid=(nf, m // tm),
        in_specs=in_specs,
        out_specs=[act_spec] + extra_specs,
        out_shape=[jax.ShapeDtypeStruct((m, d_ff), BF16)] + extra_shape,
        scratch_shapes=[pltpu.VMEM((tm + SUBLANES, tn), F32), pltpu.VMEM((tm + SUBLANES, tn), F32),
                        pltpu.VMEM((kdim, tn), BF16), pltpu.VMEM((kdim, tn), BF16)],
        compiler_params=_params(("parallel", "arbitrary")),
        name="ffn_up",
    )(*args)


def _lambda_init(layer_idx):
    return 0.8 - 0.6 * math.exp(-0.3 * layer_idx)


def _rotation_tables(pos, dk, scale):
    inv = 1.0 / (RET_ROT_BASE ** jnp.linspace(0.0, 1.0, dk // 2, dtype=F32))
    ang = pos.astype(F32)[:, None] * inv[None, :]
    cos, sin = jnp.cos(ang), jnp.sin(ang)
    cos_full = jnp.stack([cos, cos], axis=-1).reshape(pos.shape[0], dk)
    sin_signed = jnp.stack([-sin, sin], axis=-1).reshape(pos.shape[0], dk)
    return cos_full * scale, sin_signed * scale


def _ffn(x, h_rows, layer, w_up, conv_w, conv_b, w_down_bf16, *, seq_len, add=None):
    d_ff = w_down_bf16.shape[1]
    act, out_g, out_u = _ffn_up(h_rows, w_up, conv_w[layer], conv_b[layer][None, :], layer,
                                d_ff=d_ff, seq_len=seq_len, add=add)
    tk = d_ff // 2 if d_ff > 4096 else d_ff
    (y,) = _matmul(act, w_down_bf16, layer, col0=0, ncols=x.shape[1], out_dtypes=[F32],
                   epilogue=_epi_residual, extras=[x], extra_specs=[_tile_spec(min(1024, x.shape[0]), 512)],
                   tk=tk, name="ffn_down")
    return y, out_g, out_u


def kernel(x_prompt, x_sample, cache_k, cache_v, state_ret, state_ffn, page_table, norm_mix_w, norm_ffn_w, w_attn_qkv, q_norm_w, k_norm_w, lambda_q1, lambda_k1, lambda_q2, lambda_k2, subln_w, w_attn_o, w_ret_qkvg, ret_norm_w, w_ret_o, w_ffn_up, ffn_conv_w, ffn_conv_b, w_ffn_down):
    bp, sp, d = x_prompt.shape
    bs, ss, _ = x_sample.shape
    depth = norm_mix_w.shape[0]
    hd, vd = q_norm_w.shape[-1], subln_w.shape[-1]
    sub, heads = cache_k.shape[3], cache_v.shape[3]
    qw, vw = sub * hd, heads * vd
    past = page_table.shape[1] * cache_k.shape[2]
    rheads, dk, dv = state_ret.shape[2], state_ret.shape[3], state_ret.shape[4]
    rqk, rv = rheads * dk, rheads * dv
    d_ff = w_ffn_down.shape[1]
    conv_taps = ffn_conv_w.shape[1] - 1
    row = lambda w: w[None, :]

    w_down = _to_bf16(w_ffn_down)
    xp = x_prompt.reshape(bp * sp, d)
    xs = x_sample.reshape(bs * ss, d)
    kp_l, vp_l, ks_l, vs_l, rp_l, rs_l, fp_l, fs_l = [], [], [], [], [], [], [], []
    for i in range(depth):
        hp = _rmsnorm(xp, row(norm_mix_w[i]))
        hs = _rmsnorm(xs, row(norm_mix_w[i]))
        if i % 2 == 0:
            a = i // 2
            lam_init = _lambda_init(i)
            lam_rows = [row(lambda_q1[a]), row(lambda_k1[a]), row(lambda_q2[a]), row(lambda_k2[a])]
            knorm = functools.partial(_epi_headnorm, head_dim=hd, scale=1.0)
            nspec = [pl.BlockSpec((1, hd), lambda i_, j, k: (0, 0))]

            def qkv(h_rows, q_scale):
                qnorm = functools.partial(_epi_headnorm, head_dim=hd, scale=q_scale)
                (q,) = _matmul(h_rows, w_attn_qkv, a, col0=0, ncols=qw, out_dtypes=[BF16], epilogue=qnorm,
                               extras=[row(q_norm_w[a])], extra_specs=nspec, name="attn_q")
                k32, k16 = _matmul(h_rows, w_attn_qkv, a, col0=qw, ncols=qw, out_dtypes=[F32, BF16],
                                   epilogue=knorm, extras=[row(k_norm_w[a])], extra_specs=nspec,
                                   name="attn_k")
                v32, v16 = _matmul(h_rows, w_attn_qkv, a, col0=2 * qw, ncols=vw, out_dtypes=[F32, BF16],
                                   epilogue=_epi_store, name="attn_v")
                return q, k32, k16, v32, v16

            def out_proj(x, o_rows):
                (y,) = _matmul(o_rows, w_attn_o, a, col0=0, ncols=d, out_dtypes=[F32],
                               epilogue=_epi_residual, extras=[x],
                               extra_specs=[_tile_spec(min(1024, x.shape[0]), 512)], name="attn_o")
                return y

            q, k32, k16, v32, v16 = qkv(hp, hd ** -0.5 * LOG2E)
            op = _attn_prompt(q.reshape(bp, sp, qw), k16.reshape(bp, sp, qw), v16.reshape(bp, sp, vw),
                              lam_rows, row(subln_w[a]), heads=heads, head_dim=hd, v_dim=vd,
                              lam_init=lam_init)
            xp = out_proj(xp, op.reshape(bp * sp, vw))
            kp_l.append(k32.reshape(bp, sp, sub, hd))
            vp_l.append(v32.reshape(bp, sp, heads, vd))

            q, k32, _, v32, _ = qkv(hs, hd ** -0.5)
            k_new = k32.reshape(bs, ss, sub, hd)
            v_new = v32.reshape(bs, ss, heads, vd)
            qs = q.reshape(bs, ss, heads, 2, hd).transpose(0, 3, 2, 1, 4).reshape(bs, 2, heads * ss, hd)
            osm = _attn_sample(qs, k_new, v_new, cache_k, cache_v, a, page_table, lam_rows,
                               row(subln_w[a]), lam_init=lam_init)
            osm = osm.reshape(bs, heads, ss, vd).transpose(0, 2, 1, 3).reshape(bs * ss, vw)
            xs = out_proj(xs, osm)
            ks_l.append(k_new)
            vs_l.append(v_new)
        else:
            r = i // 2
            log_g = jnp.log1p(-jnp.exp2(-5.0 - jnp.arange(rheads, dtype=F32)))

            def project(h_rows, cos_q, sin_q, cos_k, sin_k, tm):
                reps = 512 // dk
                tabs = [jnp.tile(t, (1, reps)) for t in (cos_q, sin_q, cos_k, sin_k)]
                nrep = tabs[0].shape[0] // tm
                tspec = [pl.BlockSpec((tm, 512), lambda i_, j, k: (i_ % nrep, 0))] * 2
                (q,) = _matmul(h_rows, w_ret_qkvg, r, col0=0, ncols=rqk, out_dtypes=[BF16],
                               epilogue=_epi_rotate, extras=tabs[:2], extra_specs=tspec, tm=tm, name="ret_q")
                (k,) = _matmul(h_rows, w_ret_qkvg, r, col0=rqk, ncols=rqk, out_dtypes=[F32],
                               epilogue=_epi_rotate, extras=tabs[2:], extra_specs=tspec, tm=tm, name="ret_k")
                (v,) = _matmul(h_rows, w_ret_qkvg, r, col0=2 * rqk, ncols=rv, out_dtypes=[BF16],
                               epilogue=_epi_store, tm=tm, name="ret_v")
                (g,) = _matmul(h_rows, w_ret_qkvg, r, col0=2 * rqk + rv, ncols=rv, out_dtypes=[F32],
                               epilogue=_epi_store, tm=tm, name="ret_g")
                return q, k, v, g

            def out_proj(x, y_rows):
                (y,) = _matmul(y_rows, w_ret_o, r, col0=0, ncols=d, out_dtypes=[F32],
                               epilogue=_epi_residual, extras=[x],
                               extra_specs=[_tile_spec(min(1024, x.shape[0]), 512)], name="ret_o")
                return y

            tm = min(1024, sp)
            cq, sq = _rotation_tables(jnp.arange(sp), dk, 1.0)
            ck, sk = _rotation_tables(jnp.arange(sp), dk, dk ** -0.5)
            q, k, v, g = project(hp, cq, sq, ck, sk, tm)
            chunk = min(128, sp)
            y, rp = _retention(q.reshape(bp, sp, rqk), k.reshape(bp, sp, rqk), v.reshape(bp, sp, rv),
                               g.reshape(bp, sp, rv), None, row(ret_norm_w[r]), log_g,
                               heads=rheads, chunk=chunk, length=chunk)
            xp = out_proj(xp, y.reshape(bp * sp, rv))
            rp_l.append(rp)

            pos_s = past + jnp.arange(ss)
            cq, sq = (jnp.tile(t, (bs, 1)) for t in _rotation_tables(pos_s, dk, 1.0))
            ck, sk = (jnp.tile(t, (bs, 1)) for t in _rotation_tables(pos_s, dk, dk ** -0.5))
            q, k, v, g = project(hs, cq, sq, ck, sk, bs * ss)
            chunk = 128
            padded = lambda t: jnp.pad(t.reshape(bs, ss, -1), ((0, 0), (0, chunk - ss), (0, 0)))
            y, rs = _retention(padded(q), padded(k), padded(v), padded(g), state_ret[r],
                               row(ret_norm_w[r]), log_g, heads=rheads, chunk=chunk, length=ss)
            xs = out_proj(xs, y[:, :ss].reshape(bs * ss, rv))
            rs_l.append(rs)

        hp = _rmsnorm(xp, row(norm_ffn_w[i]))
        xp, st_g, st_u = _ffn(xp, hp, i, w_ffn_up, ffn_conv_w, ffn_conv_b, w_down, seq_len=sp)
        fp_l.append(jnp.concatenate([st_g, st_u], axis=-1))

        group = BF16_ROWS
        assert conv_taps + ss <= group
        hs = _rmsnorm(xs, row(norm_ffn_w[i]))
        hs_pad = jnp.pad(hs.reshape(bs, ss, d), ((0, 0), (conv_taps, group - conv_taps - ss), (0, 0)))
        add = jnp.pad(state_ffn[i], ((0, 0), (0, group - conv_taps), (0, 0)))
        xs_pad = jnp.pad(xs.reshape(bs, ss, d), ((0, 0), (conv_taps, group - conv_taps - ss), (0, 0)))
        ys_pad, ext_g, ext_u = _ffn(xs_pad.reshape(bs * group, d), hs_pad.reshape(bs * group, d), i,
                                    w_ffn_up, ffn_conv_w, ffn_conv_b, w_down,
                                    seq_len=bs * group, add=add.reshape(bs * group, 2 * d_ff))
        xs = ys_pad.reshape(bs, group, d)[:, conv_taps:conv_taps + ss].reshape(bs * ss, d)
        ext = jnp.concatenate([ext_g, ext_u], axis=-1).reshape(bs, group, 2 * d_ff)
        fs_l.append(ext[:, ss:ss + conv_taps])

    return (xp.reshape(bp, sp, d), xs.reshape(bs, ss, d),
            jnp.stack(kp_l), jnp.stack(vp_l), jnp.stack(ks_l), jnp.stack(vs_l),
            jnp.stack(rp_l), jnp.stack(rs_l), jnp.stack(fp_l), jnp.stack(fs_l))
```

```python
import functools
import math

import jax
import jax.numpy as jnp
from jax import lax
from jax.experimental import pallas as pl
from jax.experimental.pallas import tpu as pltpu

F32 = jnp.float32
BF16 = jnp.bfloat16
NORM_EPS = 1e-6
RET_ROT_BASE = 10000.0
LANES = 128
SUBLANES = 8
BF16_ROWS = 16
VMEM_LIMIT = 56 * 1024 * 1024
COMPILER_SCRATCH = 6 * 1024 * 1024
NEG = -1e30
ROW_CHUNK = 256
MM_TM, MM_TN = 2048, 256
LOG2E = math.log2(math.e)


def _dot(a, b):
    return jnp.dot(a, b, preferred_element_type=F32)


def _dot_nt(a, b):
    return lax.dot_general(a, b, (((1,), (1,)), ((), ())), preferred_element_type=F32)


def _dot_tn(a, b):
    return lax.dot_general(a, b, (((0,), (0,)), ((), ())), preferred_element_type=F32)


def _params(sem):
    return pltpu.CompilerParams(dimension_semantics=sem, vmem_limit_bytes=VMEM_LIMIT)


def _rms(x, w):
    return x * lax.rsqrt(jnp.mean(x * x, axis=-1, keepdims=True) + NORM_EPS) * w


def _as_bf16(x):
    return x if x.dtype == BF16 else x.astype(BF16)


def _rmsnorm_kernel(x_ref, w_ref, o_ref):
    o_ref[...] = _rms(x_ref[...], w_ref[...]).astype(o_ref.dtype)


def _rmsnorm(x, w_row, tr=256):
    rows, d = x.shape
    tr = min(tr, rows)
    return pl.pallas_call(
        _rmsnorm_kernel,
        grid=(rows // tr,),
        in_specs=[pl.BlockSpec((tr, d), lambda i: (i, 0)),
                  pl.BlockSpec((1, d), lambda i: (0, 0))],
        out_specs=pl.BlockSpec((tr, d), lambda i: (i, 0)),
        out_shape=jax.ShapeDtypeStruct((rows, d), BF16),
        compiler_params=_params(("parallel",)),
        name="rmsnorm",
    )(x, w_row)


def _cast_kernel(x_ref, o_ref):
    o_ref[...] = x_ref[...].astype(o_ref.dtype)


def _to_bf16(w):
    nl, kdim, n = w.shape
    tk = max(t for t in range(BF16_ROWS, 513, BF16_ROWS) if kdim % t == 0)
    return pl.pallas_call(
        _cast_kernel,
        grid=(nl, kdim // tk),
        in_specs=[pl.BlockSpec((1, tk, n), lambda l, i: (l, i, 0))],
        out_specs=pl.BlockSpec((1, tk, n), lambda l, i: (l, i, 0)),
        out_shape=jax.ShapeDtypeStruct(w.shape, BF16),
        compiler_params=_params(("parallel", "parallel")),
        name="cast_bf16",
    )(w)


def _mm_kernel(*refs, nk, n_extra, n_out, epilogue):
    a_ref, w_ref = refs[:2]
    extra = refs[2:2 + n_extra]
    outs = refs[2 + n_extra:2 + n_extra + n_out]
    w = _as_bf16(w_ref[...])
    tm = a_ref.shape[0]
    cm = min(ROW_CHUNK, tm)
    chunks = [pl.ds(c * cm, cm) for c in range(tm // cm)]
    if nk == 1:
        for rows in chunks:
            epilogue(_dot(a_ref[rows, :], w), extra, outs, rows)
        return
    acc_ref = refs[-1]
    k = pl.program_id(2)
    part = _dot(a_ref[...], w)

    @pl.when(k == 0)
    def _():
        acc_ref[...] = part

    @pl.when(k > 0)
    def _():
        acc_ref[...] += part

    @pl.when(k == nk - 1)
    def _():
        epilogue(acc_ref[...], extra, outs, pl.ds(0, tm))


def _matmul(a, w, layer, *, col0, ncols, out_dtypes, epilogue, extras=(), extra_specs=(),
            tm=MM_TM, tn=MM_TN, tk=4096, name="matmul"):
    m, kdim = a.shape
    tm, tn, tk = min(tm, m), min(tn, ncols), min(tk, kdim)
    assert m % tm == 0 and ncols % tn == 0 and col0 % tn == 0 and kdim % tk == 0
    nk = kdim // tk
    j0 = col0 // tn
    size = lambda rows, cols, dt: rows * cols * jnp.dtype(dt).itemsize
    others = (2 * size(tk, tn, w.dtype) + size(tk, tn, BF16) + sum(2 * size(tm, tn, dt) for dt in out_dtypes)
              + sum(2 * size(*spec.block_shape, e.dtype) for spec, e in zip(extra_specs, extras))
              + (size(tm, tn, F32) if nk > 1 else 0))
    fits_double = others + 2 * size(tm, tk, a.dtype) <= VMEM_LIMIT - COMPILER_SCRATCH
    a_mode = dict(pipeline_mode=pl.Buffered(1)) if nk == 1 and m > tm and not fits_double else {}
    in_specs = [pl.BlockSpec((tm, tk), lambda i, j, k: (i, k), **a_mode),
                pl.BlockSpec((None, tk, tn), lambda i, j, k: (layer, k, j0 + j))]
    in_specs += list(extra_specs)
    out_specs = [pl.BlockSpec((tm, tn), lambda i, j, k: (i, j)) for _ in out_dtypes]
    out_shape = [jax.ShapeDtypeStruct((m, ncols), dt) for dt in out_dtypes]
    scratch = [pltpu.VMEM((tm, tn), F32)] if nk > 1 else []
    return pl.pallas_call(
        functools.partial(_mm_kernel, nk=nk, n_extra=len(extras), n_out=len(out_dtypes),
                          epilogue=epilogue),
        grid=(m // tm, ncols // tn, nk),
        in_specs=in_specs, out_specs=out_specs, out_shape=out_shape,
        scratch_shapes=scratch,
        compiler_params=_params(("parallel", "parallel", "arbitrary")),
        name=name,
    )(a, w, *extras)


def _epi_store(acc, extra, outs, rows):
    for o in outs:
        o[rows, :] = acc.astype(o.dtype)


def _epi_residual(acc, extra, outs, rows):
    outs[0][rows, :] = extra[0][rows, :] + acc


def _epi_headnorm(acc, extra, outs, rows, *, head_dim, scale):
    w = extra[0][...]
    for g in range(acc.shape[1] // head_dim):
        sl = slice(g * head_dim, (g + 1) * head_dim)
        y = _rms(acc[:, sl], w)
        for o in outs:
            o[rows, sl] = y if o.dtype == F32 else (y * scale).astype(o.dtype)


def _epi_rotate(acc, extra, outs, rows):
    cos_ref, sin_ref = extra
    for g in range(acc.shape[1] // LANES):
        sl = slice(g * LANES, (g + 1) * LANES)
        x = acc[:, sl]
        lane = lax.broadcasted_iota(jnp.int32, x.shape, 1)
        swapped = jnp.where(lane % 2 == 0, pltpu.roll(x, LANES - 1, axis=1), pltpu.roll(x, 1, axis=1))
        y = x * cos_ref[rows, sl] + swapped * sin_ref[rows, sl]
        for o in outs:
            o[rows, sl] = y.astype(o.dtype)


def _tile_spec(tm, tn):
    return pl.BlockSpec((tm, tn), lambda i, j, k: (i, j))


def _diff_lambda(lq1_ref, lk1_ref, lq2_ref, lk2_ref, lam_init):
    s1 = jnp.sum(lq1_ref[...] * lk1_ref[...], axis=-1, keepdims=True)
    s2 = jnp.sum(lq2_ref[...] * lk2_ref[...], axis=-1, keepdims=True)
    return jnp.exp(s1) - jnp.exp(s2) + lam_init


def _attn_prompt_kernel(q_ref, k_ref, v_ref, lq1_ref, lk1_ref, lq2_ref, lk2_ref, sw_ref, o_ref,
                        *, head_dim, lam_init, cq):
    qi = pl.program_id(2)
    tq, s = q_ref.shape[1], k_ref.shape[1]
    n_chunks = tq // cq
    lam = _diff_lambda(lq1_ref, lk1_ref, lq2_ref, lk2_ref, lam_init)
    sw = sw_ref[...]
    tri = (lax.broadcasted_iota(jnp.int32, (cq, cq), 1) <= lax.broadcasted_iota(jnp.int32, (cq, cq), 0))

    def chunk(g):
        r0 = (g % n_chunks) * cq
        pre = g * cq
        q = q_ref[0, r0:r0 + cq, :]
        weights, sums = [], []
        for c in range(2):
            sl = slice(c * head_dim, (c + 1) * head_dim)
            sd = jnp.where(tri, _dot_nt(q[:, sl], k_ref[0, pre:pre + cq, sl]), NEG)
            m = jnp.max(sd, axis=-1, keepdims=True)
            if pre:
                sp = _dot_nt(q[:, sl], k_ref[0, 0:pre, sl])
                m = jnp.maximum(m, jnp.max(sp, axis=-1, keepdims=True))
            pd = jnp.exp2(sd - m)
            l = jnp.sum(pd, axis=-1, keepdims=True)
            pp = None
            if pre:
                pp = jnp.exp2(sp - m)
                l = l + jnp.sum(pp, axis=-1, keepdims=True)
            weights.append((pp, pd))
            sums.append(l)
        r1 = 1.0 / sums[0]
        r2 = lam / sums[1]
        o = _dot((weights[0][1] * r1 - weights[1][1] * r2).astype(BF16), v_ref[0, pre:pre + cq, :])
        if pre:
            o = o + _dot((weights[0][0] * r1 - weights[1][0] * r2).astype(BF16), v_ref[0, 0:pre, :])
        o_ref[0, r0:r0 + cq, :] = (_rms(o, sw) * (1.0 - lam_init)).astype(o_ref.dtype)

    for n in range(s // tq):
        @pl.when(qi == n)
        def _(n=n):
            for rc in range(n_chunks):
                chunk(n * n_chunks + rc)


def _attn_prompt(q, k, v, lam_rows, subln_row, *, heads, head_dim, v_dim, lam_init, tq=1024, cq=256):
    b, s, _ = q.shape
    tq = min(tq, s)
    cq = min(cq, tq)
    assert s % tq == 0 and tq % cq == 0
    vec = lambda n: pl.BlockSpec((1, n), lambda bi, h, qi: (0, 0))
    return pl.pallas_call(
        functools.partial(_attn_prompt_kernel, head_dim=head_dim, lam_init=lam_init, cq=cq),
        grid=(b, heads, s // tq),
        in_specs=[pl.BlockSpec((1, tq, 2 * head_dim), lambda bi, h, qi: (bi, qi, h)),
                  pl.BlockSpec((1, s, 2 * head_dim), lambda bi, h, qi: (bi, 0, h)),
                  pl.BlockSpec((1, s, v_dim), lambda bi, h, qi: (bi, 0, h)),
                  vec(head_dim), vec(head_dim), vec(head_dim), vec(head_dim), vec(v_dim)],
        out_specs=pl.BlockSpec((1, tq, v_dim), lambda bi, h, qi: (bi, qi, h)),
        out_shape=jax.ShapeDtypeStruct((b, s, heads * v_dim), BF16),
        compiler_params=_params(("parallel", "parallel", "parallel")),
        name="attn_prompt",
    )(q, k, v, *lam_rows, subln_row)


def _attn_sample_kernel(pt_ref, qs_ref, kn_ref, vn_ref, lq1_ref, lk1_ref, lq2_ref, lk2_ref, sw_ref,
                        *rest, pages_per_step, heads, lam_init):
    k_refs = rest[:pages_per_step]
    v_refs = rest[pages_per_step:2 * pages_per_step]
    o_ref, m_scr, l_scr, acc_scr = rest[2 * pages_per_step:]
    step = pl.program_id(1)
    t_new = kn_ref.shape[1]
    rows = 2 * heads * t_new

    def update(k_ref, v_ref, own_tokens):
        keys = k_ref.shape[0]
        cols = keys * heads
        sc = []
        for c in range(2):
            kc = k_ref[:, pl.ds(c, heads, stride=2), :].reshape(cols, k_ref.shape[2])
            sc.append(_dot_nt(qs_ref[0, c], kc.astype(BF16)))
        sc = jnp.concatenate(sc, axis=0)
        r = lax.broadcasted_iota(jnp.int32, (rows, cols), 0)
        col = lax.broadcasted_iota(jnp.int32, (rows, cols), 1)
        valid = (col % heads) == ((r // t_new) % heads)
        if own_tokens:
            valid = valid & ((col // heads) <= (r % t_new))
        sc = jnp.where(valid, sc, NEG)
        m_old = m_scr[...]
        m_new = jnp.maximum(m_old, jnp.max(sc, axis=-1, keepdims=True))
        alpha = jnp.exp(m_old - m_new)
        p = jnp.exp(sc - m_new)
        l_scr[...] = alpha * l_scr[...] + jnp.sum(p, axis=-1, keepdims=True)
        vmat = v_ref[...].reshape(cols, v_ref.shape[2])
        acc_scr[...] = alpha * acc_scr[...] + _dot(p.astype(BF16), vmat.astype(BF16))
        m_scr[...] = m_new

    @pl.when(step == 0)
    def _():
        m_scr[...] = jnp.full(m_scr.shape, NEG, F32)
        l_scr[...] = jnp.zeros(l_scr.shape, F32)
        acc_scr[...] = jnp.zeros(acc_scr.shape, F32)
        update(kn_ref.at[0], vn_ref.at[0], True)

    for g in range(pages_per_step):
        update(k_refs[g], v_refs[g], False)

    @pl.when(step == pl.num_programs(1) - 1)
    def _():
        lam = _diff_lambda(lq1_ref, lk1_ref, lq2_ref, lk2_ref, lam_init)
        o = acc_scr[...] / l_scr[...]
        half = rows // 2
        od = o[:half] - lam * o[half:]
        o_ref[0] = (_rms(od, sw_ref[...]) * (1.0 - lam_init)).astype(o_ref.dtype)


def _attn_sample(qs, k_new, v_new, cache_k, cache_v, layer, page_table, lam_rows, subln_row,
                 *, lam_init, pages_per_step=4):
    b, t_new, sub, hd = k_new.shape
    heads, vd = v_new.shape[2], v_new.shape[3]
    page = cache_k.shape[2]
    n_pages = page_table.shape[1]
    g = pages_per_step
    assert n_pages % g == 0 and heads % SUBLANES == 0
    rows = 2 * heads * t_new
    vec = lambda n: pl.BlockSpec((1, n), lambda bi, s, pt: (0, 0))

    def page_spec(shape, gi):
        return pl.BlockSpec((None, None) + shape, lambda bi, s, pt: (layer, pt[bi, s * g + gi], 0, 0, 0))

    in_specs = [pl.BlockSpec((1, 2, heads * t_new, hd), lambda bi, s, pt: (bi, 0, 0, 0)),
                pl.BlockSpec((1, t_new, sub, hd), lambda bi, s, pt: (bi, 0, 0, 0)),
                pl.BlockSpec((1, t_new, heads, vd), lambda bi, s, pt: (bi, 0, 0, 0)),
                vec(hd), vec(hd), vec(hd), vec(hd), vec(vd)]
    in_specs += [page_spec((page, sub, hd), gi) for gi in range(g)]
    in_specs += [page_spec((page, heads, vd), gi) for gi in range(g)]
    return pl.pallas_call(
        functools.partial(_attn_sample_kernel, pages_per_step=g, heads=heads, lam_init=lam_init),
        grid_spec=pltpu.PrefetchScalarGridSpec(
            num_scalar_prefetch=1,
            grid=(b, n_pages // g),
            in_specs=in_specs,
            out_specs=pl.BlockSpec((1, heads * t_new, vd), lambda bi, s, pt: (bi, 0, 0)),
            scratch_shapes=[pltpu.VMEM((rows, 1), F32), pltpu.VMEM((rows, 1), F32),
                            pltpu.VMEM((rows, vd), F32)]),
        out_shape=jax.ShapeDtypeStruct((b, heads * t_new, vd), BF16),
        compiler_params=_params(("parallel", "arbitrary")),
        name="attn_sample",
    )(page_table, qs, k_new, v_new, *lam_rows, subln_row, *([cache_k] * g), *([cache_v] * g))


def _retention_kernel(lg_ref, gl_ref, q_ref, k_ref, v_ref, g_ref, *rest, length, has_state, hps):
    if has_state:
        r0_ref, nw_ref, y_ref, rout_ref, r_scr = rest
    else:
        nw_ref, y_ref, rout_ref, r_scr = rest
    hg, c = pl.program_id(1), pl.program_id(2)

    @pl.when(c == 0)
    def _():
        r_scr[...] = r0_ref[0] if has_state else jnp.zeros(r_scr.shape, F32)

    lp = q_ref.shape[1]
    dk, dv = r_scr.shape[1], r_scr.shape[2]
    rel = (lax.broadcasted_iota(jnp.int32, (lp, lp), 0)
           - lax.broadcasted_iota(jnp.int32, (lp, lp), 1)).astype(F32)
    idx = lax.broadcasted_iota(jnp.int32, (lp, 1), 0).astype(F32)
    nw = nw_ref[...]
    for hh in range(hps):
        h = hg * hps + hh
        lg = lg_ref[h]
        ks, vs = slice(hh * dk, (hh + 1) * dk), slice(hh * dv, (hh + 1) * dv)
        q, k, v = q_ref[0, :, ks], k_ref[0, :, ks], v_ref[0, :, vs]
        dmask = jnp.where(rel >= 0, jnp.exp(jnp.maximum(rel, 0.0) * lg), 0.0)
        inner = _dot_nt(q, k.astype(BF16)) * dmask
        r_old = r_scr[hh]
        o = _dot(inner.astype(BF16), v) + _dot(q, r_old.astype(BF16)) * jnp.exp((idx + 1.0) * lg)
        kdec = k * jnp.exp((length - 1.0 - idx) * lg)
        r_scr[hh] = r_old * gl_ref[h] + _dot_tn(kdec.astype(BF16), v)
        g = g_ref[0, :, vs]
        y_ref[0, :, vs] = (g * jax.nn.sigmoid(g) * _rms(o, nw)).astype(y_ref.dtype)

    @pl.when(c == pl.num_programs(2) - 1)
    def _():
        rout_ref[0] = r_scr[...]


def _retention(q, k, v, g, state, norm_row, log_g, *, heads, chunk, length, hps=4):
    b, s, _ = q.shape
    dk, dv = q.shape[2] // heads, v.shape[2] // heads
    hps = min(hps, heads)
    assert heads % hps == 0
    decay_len = jnp.exp(length * log_g)
    seq = lambda n: pl.BlockSpec((1, chunk, hps * n), lambda bi, h, c, *_: (bi, c, h))
    st = pl.BlockSpec((1, hps, dk, dv), lambda bi, h, c, *_: (bi, h, 0, 0))
    in_specs = [seq(dk), seq(dk), seq(dv), seq(dv)]
    args = [q, k, v, g]
    if state is not None:
        in_specs.append(st)
        args.append(state)
    in_specs.append(pl.BlockSpec((1, dv), lambda bi, h, c, *_: (0, 0)))
    args.append(norm_row)
    return pl.pallas_call(
        functools.partial(_retention_kernel, length=float(length), has_state=state is not None, hps=hps),
        grid_spec=pltpu.PrefetchScalarGridSpec(
            num_scalar_prefetch=2,
            grid=(b, heads // hps, s // chunk),
            in_specs=in_specs,
            out_specs=[seq(dv), st],
            scratch_shapes=[pltpu.VMEM((hps, dk, dv), F32)]),
        out_shape=[jax.ShapeDtypeStruct((b, s, heads * dv), BF16),
                   jax.ShapeDtypeStruct((b, heads, dk, dv), F32)],
        compiler_params=_params(("parallel", "parallel", "arbitrary")),
        name="retention",
    )(log_g, decay_len, *args)


def _ffn_up_kernel(a_ref, wg_ref, wu_ref, cwg_ref, cwu_ref, cbg_ref, cbu_ref, *rest,
                   tiles_per_seq, has_add):
    if has_add:
        addg_ref, addu_ref, act_ref, outg_ref, outu_ref, eg_scr, eu_scr = rest
    else:
        act_ref, outg_ref, outu_ref, eg_scr, eu_scr = rest
        addg_ref = addu_ref = None
    i = pl.program_id(1)
    tm = a_ref.shape[0]
    pad = SUBLANES

    @pl.when(i % tiles_per_seq == 0)
    def _():
        eg_scr[0:pad, :] = jnp.zeros((pad, eg_scr.shape[1]), F32)
        eu_scr[0:pad, :] = jnp.zeros((pad, eu_scr.shape[1]), F32)

    a = a_ref[...]

    def half(w_ref, cw_ref, cb_ref, add_ref, e_scr, out_ref):
        u = _dot(a, w_ref[...].astype(BF16))
        if has_add:
            u = u + add_ref[...]
            out_ref[...] = u
        e_scr[pad:pad + tm, :] = u
        conv = (cb_ref[...] + e_scr[pad - 2:pad - 2 + tm, :] * cw_ref[0:1, :]
                + e_scr[pad - 1:pad - 1 + tm, :] * cw_ref[1:2, :] + u * cw_ref[2:3, :])
        if not has_add:
            out_ref[0] = e_scr[pad + tm - 2:pad + tm, :]
        e_scr[0:pad, :] = e_scr[tm:tm + pad, :]
        return conv

    gate = half(wg_ref, cwg_ref, cbg_ref, addg_ref, eg_scr, outg_ref)
    up = half(wu_ref, cwu_ref, cbu_ref, addu_ref, eu_scr, outu_ref)
    act_ref[...] = (gate * jax.nn.sigmoid(gate) * up).astype(act_ref.dtype)


def _ffn_up(a, w_up, conv_w, conv_b, layer, *, d_ff, seq_len, add=None, tm=1024, tn=256):
    m, kdim = a.shape
    tm, tn = min(tm, m, seq_len), min(tn, d_ff)
    assert m % tm == 0 and d_ff % tn == 0 and seq_len % tm == 0
    nf = d_ff // tn
    has_add = add is not None
    wspec = lambda off: pl.BlockSpec((None, kdim, tn), lambda j, i: (layer, 0, off + j))
    cwspec = lambda off: pl.BlockSpec((conv_w.shape[0], tn), lambda j, i: (0, off + j))
    cbspec = lambda off: pl.BlockSpec((1, tn), lambda j, i: (0, off + j))
    in_specs = [pl.BlockSpec((tm, kdim), lambda j, i: (i, 0)), wspec(0), wspec(nf),
                cwspec(0), cwspec(nf), cbspec(0), cbspec(nf)]
    args = [a, w_up, w_up, conv_w, conv_w, conv_b, conv_b]
    act_spec = pl.BlockSpec((tm, tn), lambda j, i: (i, j))
    if has_add:
        assert m == tm
        in_specs += [pl.BlockSpec((tm, tn), lambda j, i: (i, j)),
                     pl.BlockSpec((tm, tn), lambda j, i: (i, nf + j))]
        args += [add, add]
        extra_specs = [act_spec, act_spec]
        extra_shape = [jax.ShapeDtypeStruct((m, d_ff), F32)] * 2
    else:
        tps = seq_len // tm
        st_spec = pl.BlockSpec((1, 2, tn), lambda j, i: (i // tps, 0, j))
        extra_specs = [st_spec, st_spec]
        extra_shape = [jax.ShapeDtypeStruct((m // seq_len, 2, d_ff), F32)] * 2
    return pl.pallas_call(
        functools.partial(_ffn_up_kernel, tiles_per_seq=seq_len // tm, has_add=has_add),
        grid=(nf, m // tm),
        in_specs=in_specs,
        out_specs=[act_spec] + extra_specs,
        out_shape=[jax.ShapeDtypeStruct((m, d_ff), BF16)] + extra_shape,
        scratch_shapes=[pltpu.VMEM((tm + SUBLANES, tn), F32), pltpu.VMEM((tm + SUBLANES, tn), F32)],
        compiler_params=_params(("parallel", "arbitrary")),
        name="ffn_up",
    )(*args)


def _lambda_init(layer_idx):
    return 0.8 - 0.6 * math.exp(-0.3 * layer_idx)


def _rotation_tables(pos, dk, scale):
    inv = 1.0 / (RET_ROT_BASE ** jnp.linspace(0.0, 1.0, dk // 2, dtype=F32))
    ang = pos.astype(F32)[:, None] * inv[None, :]
    cos, sin = jnp.cos(ang), jnp.sin(ang)
    cos_full = jnp.stack([cos, cos], axis=-1).reshape(pos.shape[0], dk)
    sin_signed = jnp.stack([-sin, sin], axis=-1).reshape(pos.shape[0], dk)
    return cos_full * scale, sin_signed * scale


def _ffn(x, h_rows, layer, w_up, conv_w, conv_b, w_down_bf16, *, seq_len, add=None):
    d_ff = w_down_bf16.shape[1]
    act, out_g, out_u = _ffn_up(h_rows, w_up, conv_w[layer], conv_b[layer][None, :], layer,
                                d_ff=d_ff, seq_len=seq_len, add=add)
    tk = d_ff // 2 if d_ff > 4096 else d_ff
    (y,) = _matmul(act, w_down_bf16, layer, col0=0, ncols=x.shape[1], out_dtypes=[F32],
                   epilogue=_epi_residual, extras=[x], extra_specs=[_tile_spec(min(1024, x.shape[0]), 512)],
                   tm=1024, tn=512, tk=tk, name="ffn_down")
    return y, out_g, out_u


def kernel(x_prompt, x_sample, cache_k, cache_v, state_ret, state_ffn, page_table, norm_mix_w, norm_ffn_w, w_attn_qkv, q_norm_w, k_norm_w, lambda_q1, lambda_k1, lambda_q2, lambda_k2, subln_w, w_attn_o, w_ret_qkvg, ret_norm_w, w_ret_o, w_ffn_up, ffn_conv_w, ffn_conv_b, w_ffn_down):
    bp, sp, d = x_prompt.shape
    bs, ss, _ = x_sample.shape
    depth = norm_mix_w.shape[0]
    hd, vd = q_norm_w.shape[-1], subln_w.shape[-1]
    sub, heads = cache_k.shape[3], cache_v.shape[3]
    qw, vw = sub * hd, heads * vd
    past = page_table.shape[1] * cache_k.shape[2]
    rheads, dk, dv = state_ret.shape[2], state_ret.shape[3], state_ret.shape[4]
    rqk, rv = rheads * dk, rheads * dv
    d_ff = w_ffn_down.shape[1]
    conv_taps = ffn_conv_w.shape[1] - 1
    row = lambda w: w[None, :]

    w_down = _to_bf16(w_ffn_down)
    w_ret_o_bf16 = _to_bf16(w_ret_o)
    xp = x_prompt.reshape(bp * sp, d)
    xs = x_sample.reshape(bs * ss, d)
    kp_l, vp_l, ks_l, vs_l, rp_l, rs_l, fp_l, fs_l = [], [], [], [], [], [], [], []
    for i in range(depth):
        hp = _rmsnorm(xp, row(norm_mix_w[i]))
        hs = _rmsnorm(xs, row(norm_mix_w[i]))
        if i % 2 == 0:
            a = i // 2
            lam_init = _lambda_init(i)
            lam_rows = [row(lambda_q1[a]), row(lambda_k1[a]), row(lambda_q2[a]), row(lambda_k2[a])]
            knorm = functools.partial(_epi_headnorm, head_dim=hd, scale=1.0)
            nspec = [pl.BlockSpec((1, hd), lambda i_, j, k: (0, 0))]

            def qkv(h_rows, q_scale):
                qnorm = functools.partial(_epi_headnorm, head_dim=hd, scale=q_scale)
                (q,) = _matmul(h_rows, w_attn_qkv, a, col0=0, ncols=qw, out_dtypes=[BF16], epilogue=qnorm,
                               extras=[row(q_norm_w[a])], extra_specs=nspec, name="attn_q")
                k32, k16 = _matmul(h_rows, w_attn_qkv, a, col0=qw, ncols=qw, out_dtypes=[F32, BF16],
                                   epilogue=knorm, extras=[row(k_norm_w[a])], extra_specs=nspec,
                                   name="attn_k")
                v32, v16 = _matmul(h_rows, w_attn_qkv, a, col0=2 * qw, ncols=vw, out_dtypes=[F32, BF16],
                                   epilogue=_epi_store, name="attn_v")
                return q, k32, k16, v32, v16

            def out_proj(x, o_rows):
                (y,) = _matmul(o_rows, w_attn_o, a, col0=0, ncols=d, out_dtypes=[F32],
                               epilogue=_epi_residual, extras=[x],
                               extra_specs=[_tile_spec(min(MM_TM, x.shape[0]), MM_TN)], name="attn_o")
                return y

            q, k32, k16, v32, v16 = qkv(hp, hd ** -0.5 * LOG2E)
            op = _attn_prompt(q.reshape(bp, sp, qw), k16.reshape(bp, sp, qw), v16.reshape(bp, sp, vw),
                              lam_rows, row(subln_w[a]), heads=heads, head_dim=hd, v_dim=vd,
                              lam_init=lam_init)
            xp = out_proj(xp, op.reshape(bp * sp, vw))
            kp_l.append(k32.reshape(bp, sp, sub, hd))
            vp_l.append(v32.reshape(bp, sp, heads, vd))

            q, k32, _, v32, _ = qkv(hs, hd ** -0.5)
            k_new = k32.reshape(bs, ss, sub, hd)
            v_new = v32.reshape(bs, ss, heads, vd)
            qs = q.reshape(bs, ss, heads, 2, hd).transpose(0, 3, 2, 1, 4).reshape(bs, 2, heads * ss, hd)
            osm = _attn_sample(qs, k_new, v_new, cache_k, cache_v, a, page_table, lam_rows,
                               row(subln_w[a]), lam_init=lam_init)
            osm = osm.reshape(bs, heads, ss, vd).transpose(0, 2, 1, 3).reshape(bs * ss, vw)
            xs = out_proj(xs, osm)
            ks_l.append(k_new)
            vs_l.append(v_new)
        else:
            r = i // 2
            log_g = jnp.log1p(-jnp.exp2(-5.0 - jnp.arange(rheads, dtype=F32)))

            def project(h_rows, cos_q, sin_q, cos_k, sin_k, tm):
                reps = MM_TN // dk
                tabs = [jnp.tile(t, (1, reps)) for t in (cos_q, sin_q, cos_k, sin_k)]
                nrep = tabs[0].shape[0] // tm
                tspec = [pl.BlockSpec((tm, MM_TN), lambda i_, j, k: (i_ % nrep, 0))] * 2
                (q,) = _matmul(h_rows, w_ret_qkvg, r, col0=0, ncols=rqk, out_dtypes=[BF16],
                               epilogue=_epi_rotate, extras=tabs[:2], extra_specs=tspec, tm=tm, name="ret_q")
                (k,) = _matmul(h_rows, w_ret_qkvg, r, col0=rqk, ncols=rqk, out_dtypes=[F32],
                               epilogue=_epi_rotate, extras=tabs[2:], extra_specs=tspec, tm=tm, name="ret_k")
                (v,) = _matmul(h_rows, w_ret_qkvg, r, col0=2 * rqk, ncols=rv, out_dtypes=[BF16],
                               epilogue=_epi_store, tm=tm, name="ret_v")
                (g,) = _matmul(h_rows, w_ret_qkvg, r, col0=2 * rqk + rv, ncols=rv, out_dtypes=[F32],
                               epilogue=_epi_store, tm=tm, name="ret_g")
                return q, k, v, g

            def out_proj(x, y_rows):
                (y,) = _matmul(y_rows, w_ret_o_bf16, r, col0=0, ncols=d, out_dtypes=[F32],
                               epilogue=_epi_residual, extras=[x],
                               extra_specs=[_tile_spec(min(1024, x.shape[0]), MM_TN)],
                               tm=1024, tk=rv, name="ret_o")
                return y

            tm = min(MM_TM, sp)
            cq, sq = _rotation_tables(jnp.arange(sp), dk, 1.0)
            ck, sk = _rotation_tables(jnp.arange(sp), dk, dk ** -0.5)
            q, k, v, g = project(hp, cq, sq, ck, sk, tm)
            chunk = min(128, sp)
            y, rp = _retention(q.reshape(bp, sp, rqk), k.reshape(bp, sp, rqk), v.reshape(bp, sp, rv),
                               g.reshape(bp, sp, rv), None, row(ret_norm_w[r]), log_g,
                               heads=rheads, chunk=chunk, length=chunk)
            xp = out_proj(xp, y.reshape(bp * sp, rv))
            rp_l.append(rp)

            pos_s = past + jnp.arange(ss)
            cq, sq = (jnp.tile(t, (bs, 1)) for t in _rotation_tables(pos_s, dk, 1.0))
            ck, sk = (jnp.tile(t, (bs, 1)) for t in _rotation_tables(pos_s, dk, dk ** -0.5))
            q, k, v, g = project(hs, cq, sq, ck, sk, bs * ss)
            chunk = 128
            padded = lambda t: jnp.pad(t.reshape(bs, ss, -1), ((0, 0), (0, chunk - ss), (0, 0)))
            y, rs = _retention(padded(q), padded(k), padded(v), padded(g), state_ret[r],
                               row(ret_norm_w[r]), log_g, heads=rheads, chunk=chunk, length=ss)
            xs = out_proj(xs, y[:, :ss].reshape(bs * ss, rv))
            rs_l.append(rs)

        hp = _rmsnorm(xp, row(norm_ffn_w[i]))
        xp, st_g, st_u = _ffn(xp, hp, i, w_ffn_up, ffn_conv_w, ffn_conv_b, w_down, seq_len=sp)
        fp_l.append(jnp.concatenate([st_g, st_u], axis=-1))

        group = BF16_ROWS
        assert conv_taps + ss <= group
        hs = _rmsnorm(xs, row(norm_ffn_w[i]))
        hs_pad = jnp.pad(hs.reshape(bs, ss, d), ((0, 0), (conv_taps, group - conv_taps - ss), (0, 0)))
        add = jnp.pad(state_ffn[i], ((0, 0), (0, group - conv_taps), (0, 0)))
        xs_pad = jnp.pad(xs.reshape(bs, ss, d), ((0, 0), (conv_taps, group - conv_taps - ss), (0, 0)))
        ys_pad, ext_g, ext_u = _ffn(xs_pad.reshape(bs * group, d), hs_pad.reshape(bs * group, d), i,
                                    w_ffn_up, ffn_conv_w, ffn_conv_b, w_down,
                                    seq_len=bs * group, add=add.reshape(bs * group, 2 * d_ff))
        xs = ys_pad.reshape(bs, group, d)[:, conv_taps:conv_taps + ss].reshape(bs * ss, d)
        ext = jnp.concatenate([ext_g, ext_u], axis=-1).reshape(bs, group, 2 * d_ff)
        fs_l.append(ext[:, ss:ss + conv_taps])

    return (xp.reshape(bp, sp, d), xs.reshape(bs, ss, d),
            jnp.stack(kp_l), jnp.stack(vp_l), jnp.stack(ks_l), jnp.stack(vs_l),
            jnp.stack(rp_l), jnp.stack(rs_l), jnp.stack(fp_l), jnp.stack(fs_l))
```

```python
import functools
import math

import jax
import jax.numpy as jnp
from jax import lax
from jax.experimental import pallas as pl
from jax.experimental.pallas import tpu as pltpu

F32 = jnp.float32
BF16 = jnp.bfloat16
NORM_EPS = 1e-6
RET_ROT_BASE = 10000.0
LANES = 128
SUBLANES = 8
BF16_ROWS = 16
VMEM_LIMIT = 56 * 1024 * 1024
COMPILER_SCRATCH = 6 * 1024 * 1024
NEG = -1e30
ROW_CHUNK = 256
MM_TM, MM_TN = 2048, 256
LOG2E = math.log2(math.e)


def _dot(a, b):
    return jnp.dot(a, b, preferred_element_type=F32)


def _dot_nt(a, b):
    return lax.dot_general(a, b, (((1,), (1,)), ((), ())), preferred_element_type=F32)


def _dot_tn(a, b):
    return lax.dot_general(a, b, (((0,), (0,)), ((), ())), preferred_element_type=F32)


def _params(sem):
    return pltpu.CompilerParams(dimension_semantics=sem, vmem_limit_bytes=VMEM_LIMIT)


def _rms(x, w):
    return x * lax.rsqrt(jnp.mean(x * x, axis=-1, keepdims=True) + NORM_EPS) * w


def _as_bf16(x):
    return x if x.dtype == BF16 else x.astype(BF16)


def _rmsnorm_kernel(x_ref, w_ref, o_ref):
    o_ref[...] = _rms(x_ref[...], w_ref[...]).astype(o_ref.dtype)


def _rmsnorm(x, w_row, tr=512):
    rows, d = x.shape
    tr = min(tr, rows)
    return pl.pallas_call(
        _rmsnorm_kernel,
        grid=(rows // tr,),
        in_specs=[pl.BlockSpec((tr, d), lambda i: (i, 0)),
                  pl.BlockSpec((1, d), lambda i: (0, 0))],
        out_specs=pl.BlockSpec((tr, d), lambda i: (i, 0)),
        out_shape=jax.ShapeDtypeStruct((rows, d), BF16),
        compiler_params=_params(("parallel",)),
        name="rmsnorm",
    )(x, w_row)


def _cast_kernel(x_ref, o_ref):
    o_ref[...] = x_ref[...].astype(o_ref.dtype)


def _to_bf16(w):
    nl, kdim, n = w.shape
    tk = max(t for t in range(BF16_ROWS, 513, BF16_ROWS) if kdim % t == 0)
    return pl.pallas_call(
        _cast_kernel,
        grid=(nl, kdim // tk),
        in_specs=[pl.BlockSpec((1, tk, n), lambda l, i: (l, i, 0))],
        out_specs=pl.BlockSpec((1, tk, n), lambda l, i: (l, i, 0)),
        out_shape=jax.ShapeDtypeStruct(w.shape, BF16),
        compiler_params=_params(("parallel", "parallel")),
        name="cast_bf16",
    )(w)


def _mm_kernel(*refs, nk, n_extra, n_out, epilogue):
    a_ref, w_ref = refs[:2]
    extra = refs[2:2 + n_extra]
    outs = refs[2 + n_extra:2 + n_extra + n_out]
    w = _as_bf16(w_ref[...])
    tm = a_ref.shape[0]
    cm = min(ROW_CHUNK, tm)
    chunks = [pl.ds(c * cm, cm) for c in range(tm // cm)]
    if nk == 1:
        for rows in chunks:
            epilogue(_dot(a_ref[rows, :], w), extra, outs, rows)
        return
    acc_ref = refs[-1]
    k = pl.program_id(2)
    part = _dot(a_ref[...], w)

    @pl.when(k == 0)
    def _():
        acc_ref[...] = part

    @pl.when(k > 0)
    def _():
        acc_ref[...] += part

    @pl.when(k == nk - 1)
    def _():
        epilogue(acc_ref[...], extra, outs, pl.ds(0, tm))


def _matmul(a, w, layer, *, col0, ncols, out_dtypes, epilogue, extras=(), extra_specs=(),
            tm=MM_TM, tn=MM_TN, tk=4096, name="matmul"):
    m, kdim = a.shape
    tm, tn, tk = min(tm, m), min(tn, ncols), min(tk, kdim)
    assert m % tm == 0 and ncols % tn == 0 and col0 % tn == 0 and kdim % tk == 0
    nk = kdim // tk
    j0 = col0 // tn
    size = lambda rows, cols, dt: rows * cols * jnp.dtype(dt).itemsize
    others = (2 * size(tk, tn, w.dtype) + size(tk, tn, BF16) + sum(2 * size(tm, tn, dt) for dt in out_dtypes)
              + sum(2 * size(*spec.block_shape, e.dtype) for spec, e in zip(extra_specs, extras))
              + (size(tm, tn, F32) if nk > 1 else 0))
    fits_double = others + 2 * size(tm, tk, a.dtype) <= VMEM_LIMIT - COMPILER_SCRATCH
    a_mode = dict(pipeline_mode=pl.Buffered(1)) if nk == 1 and m > tm and not fits_double else {}
    in_specs = [pl.BlockSpec((tm, tk), lambda i, j, k: (i, k), **a_mode),
                pl.BlockSpec((None, tk, tn), lambda i, j, k: (layer, k, j0 + j))]
    in_specs += list(extra_specs)
    out_specs = [pl.BlockSpec((tm, tn), lambda i, j, k: (i, j)) for _ in out_dtypes]
    out_shape = [jax.ShapeDtypeStruct((m, ncols), dt) for dt in out_dtypes]
    scratch = [pltpu.VMEM((tm, tn), F32)] if nk > 1 else []
    return pl.pallas_call(
        functools.partial(_mm_kernel, nk=nk, n_extra=len(extras), n_out=len(out_dtypes),
                          epilogue=epilogue),
        grid=(m // tm, ncols // tn, nk),
        in_specs=in_specs, out_specs=out_specs, out_shape=out_shape,
        scratch_shapes=scratch,
        compiler_params=_params(("parallel", "parallel", "arbitrary")),
        name=name,
    )(a, w, *extras)


def _epi_store(acc, extra, outs, rows):
    for o in outs:
        o[rows, :] = acc.astype(o.dtype)


def _epi_residual(acc, extra, outs, rows):
    outs[0][rows, :] = extra[0][rows, :] + acc


def _epi_headnorm(acc, extra, outs, rows, *, head_dim, scale):
    w = extra[0][...]
    for g in range(acc.shape[1] // head_dim):
        sl = slice(g * head_dim, (g + 1) * head_dim)
        y = _rms(acc[:, sl], w)
        for o in outs:
            o[rows, sl] = y if o.dtype == F32 else (y * scale).astype(o.dtype)


def _epi_rotate(acc, extra, outs, rows):
    cos_ref, sin_ref = extra
    for g in range(acc.shape[1] // LANES):
        sl = slice(g * LANES, (g + 1) * LANES)
        x = acc[:, sl]
        lane = lax.broadcasted_iota(jnp.int32, x.shape, 1)
        swapped = jnp.where(lane % 2 == 0, pltpu.roll(x, LANES - 1, axis=1), pltpu.roll(x, 1, axis=1))
        y = x * cos_ref[rows, sl] + swapped * sin_ref[rows, sl]
        for o in outs:
            o[rows, sl] = y.astype(o.dtype)


def _tile_spec(tm, tn):
    return pl.BlockSpec((tm, tn), lambda i, j, k: (i, j))


def _diff_lambda(lq1_ref, lk1_ref, lq2_ref, lk2_ref, lam_init):
    s1 = jnp.sum(lq1_ref[...] * lk1_ref[...], axis=-1, keepdims=True)
    s2 = jnp.sum(lq2_ref[...] * lk2_ref[...], axis=-1, keepdims=True)
    return jnp.exp(s1) - jnp.exp(s2) + lam_init


def _attn_prompt_kernel(q_ref, k_ref, v_ref, lq1_ref, lk1_ref, lq2_ref, lk2_ref, sw_ref, o_ref,
                        *, head_dim, lam_init, cq):
    qi = pl.program_id(2)
    tq, s = q_ref.shape[1], k_ref.shape[1]
    n_chunks = tq // cq
    lam = _diff_lambda(lq1_ref, lk1_ref, lq2_ref, lk2_ref, lam_init)
    sw = sw_ref[...]
    tri = (lax.broadcasted_iota(jnp.int32, (cq, cq), 1) <= lax.broadcasted_iota(jnp.int32, (cq, cq), 0))

    def chunk(g):
        r0 = (g % n_chunks) * cq
        pre = g * cq
        q = q_ref[0, r0:r0 + cq, :]
        weights, sums = [], []
        for c in range(2):
            sl = slice(c * head_dim, (c + 1) * head_dim)
            sd = jnp.where(tri, _dot_nt(q[:, sl], k_ref[0, pre:pre + cq, sl]), NEG)
            m = jnp.max(sd, axis=-1, keepdims=True)
            if pre:
                sp = _dot_nt(q[:, sl], k_ref[0, 0:pre, sl])
                m = jnp.maximum(m, jnp.max(sp, axis=-1, keepdims=True))
            pd = jnp.exp2(sd - m)
            l = jnp.sum(pd, axis=-1, keepdims=True)
            pp = None
            if pre:
                pp = jnp.exp2(sp - m)
                l = l + jnp.sum(pp, axis=-1, keepdims=True)
            weights.append((pp, pd))
            sums.append(l)
        r1 = 1.0 / sums[0]
        r2 = lam / sums[1]
        o = _dot((weights[0][1] * r1 - weights[1][1] * r2).astype(BF16), v_ref[0, pre:pre + cq, :])
        if pre:
            o = o + _dot((weights[0][0] * r1 - weights[1][0] * r2).astype(BF16), v_ref[0, 0:pre, :])
        o_ref[0, r0:r0 + cq, :] = (_rms(o, sw) * (1.0 - lam_init)).astype(o_ref.dtype)

    for n in range(s // tq):
        @pl.when(qi == n)
        def _(n=n):
            for rc in range(n_chunks):
                chunk(n * n_chunks + rc)


def _attn_prompt(q, k, v, lam_rows, subln_row, *, heads, head_dim, v_dim, lam_init, tq=1024, cq=256):
    b, s, _ = q.shape
    tq = min(tq, s)
    cq = min(cq, tq)
    assert s % tq == 0 and tq % cq == 0
    vec = lambda n: pl.BlockSpec((1, n), lambda bi, h, qi: (0, 0))
    return pl.pallas_call(
        functools.partial(_attn_prompt_kernel, head_dim=head_dim, lam_init=lam_init, cq=cq),
        grid=(b, heads, s // tq),
        in_specs=[pl.BlockSpec((1, tq, 2 * head_dim), lambda bi, h, qi: (bi, qi, h)),
                  pl.BlockSpec((1, s, 2 * head_dim), lambda bi, h, qi: (bi, 0, h)),
                  pl.BlockSpec((1, s, v_dim), lambda bi, h, qi: (bi, 0, h)),
                  vec(head_dim), vec(head_dim), vec(head_dim), vec(head_dim), vec(v_dim)],
        out_specs=pl.BlockSpec((1, tq, v_dim), lambda bi, h, qi: (bi, qi, h)),
        out_shape=jax.ShapeDtypeStruct((b, s, heads * v_dim), BF16),
        compiler_params=_params(("parallel", "parallel", "parallel")),
        name="attn_prompt",
    )(q, k, v, *lam_rows, subln_row)


def _attn_sample_kernel(pt_ref, qs_ref, kn_ref, vn_ref, lq1_ref, lk1_ref, lq2_ref, lk2_ref, sw_ref,
                        *rest, pages_per_step, heads, lam_init):
    k_refs = rest[:pages_per_step]
    v_refs = rest[pages_per_step:2 * pages_per_step]
    o_ref, m_scr, l_scr, acc_scr = rest[2 * pages_per_step:]
    step = pl.program_id(1)
    t_new = kn_ref.shape[1]
    rows = 2 * heads * t_new

    def update(k_ref, v_ref, own_tokens):
        keys = k_ref.shape[0]
        cols = keys * heads
        sc = []
        for c in range(2):
            kc = k_ref[:, pl.ds(c, heads, stride=2), :].reshape(cols, k_ref.shape[2])
            sc.append(_dot_nt(qs_ref[0, c], kc.astype(BF16)))
        sc = jnp.concatenate(sc, axis=0)
        r = lax.broadcasted_iota(jnp.int32, (rows, cols), 0)
        col = lax.broadcasted_iota(jnp.int32, (rows, cols), 1)
        valid = (col % heads) == ((r // t_new) % heads)
        if own_tokens:
            valid = valid & ((col // heads) <= (r % t_new))
        sc = jnp.where(valid, sc, NEG)
        m_old = m_scr[...]
        m_new = jnp.maximum(m_old, jnp.max(sc, axis=-1, keepdims=True))
        alpha = jnp.exp(m_old - m_new)
        p = jnp.exp(sc - m_new)
        l_scr[...] = alpha * l_scr[...] + jnp.sum(p, axis=-1, keepdims=True)
        vmat = v_ref[...].reshape(cols, v_ref.shape[2])
        acc_scr[...] = alpha * acc_scr[...] + _dot(p.astype(BF16), vmat.astype(BF16))
        m_scr[...] = m_new

    @pl.when(step == 0)
    def _():
        m_scr[...] = jnp.full(m_scr.shape, NEG, F32)
        l_scr[...] = jnp.zeros(l_scr.shape, F32)
        acc_scr[...] = jnp.zeros(acc_scr.shape, F32)
        update(kn_ref.at[0], vn_ref.at[0], True)

    for g in range(pages_per_step):
        update(k_refs[g], v_refs[g], False)

    @pl.when(step == pl.num_programs(1) - 1)
    def _():
        lam = _diff_lambda(lq1_ref, lk1_ref, lq2_ref, lk2_ref, lam_init)
        o = acc_scr[...] / l_scr[...]
        half = rows // 2
        od = o[:half] - lam * o[half:]
        o_ref[0] = (_rms(od, sw_ref[...]) * (1.0 - lam_init)).astype(o_ref.dtype)


def _attn_sample(qs, k_new, v_new, cache_k, cache_v, layer, page_table, lam_rows, subln_row,
                 *, lam_init, pages_per_step=4):
    b, t_new, sub, hd = k_new.shape
    heads, vd = v_new.shape[2], v_new.shape[3]
    page = cache_k.shape[2]
    n_pages = page_table.shape[1]
    g = pages_per_step
    assert n_pages % g == 0 and heads % SUBLANES == 0
    rows = 2 * heads * t_new
    vec = lambda n: pl.BlockSpec((1, n), lambda bi, s, pt: (0, 0))

    def page_spec(shape, gi):
        return pl.BlockSpec((None, None) + shape, lambda bi, s, pt: (layer, pt[bi, s * g + gi], 0, 0, 0))

    in_specs = [pl.BlockSpec((1, 2, heads * t_new, hd), lambda bi, s, pt: (bi, 0, 0, 0)),
                pl.BlockSpec((1, t_new, sub, hd), lambda bi, s, pt: (bi, 0, 0, 0)),
                pl.BlockSpec((1, t_new, heads, vd), lambda bi, s, pt: (bi, 0, 0, 0)),
                vec(hd), vec(hd), vec(hd), vec(hd), vec(vd)]
    in_specs += [page_spec((page, sub, hd), gi) for gi in range(g)]
    in_specs += [page_spec((page, heads, vd), gi) for gi in range(g)]
    return pl.pallas_call(
        functools.partial(_attn_sample_kernel, pages_per_step=g, heads=heads, lam_init=lam_init),
        grid_spec=pltpu.PrefetchScalarGridSpec(
            num_scalar_prefetch=1,
            grid=(b, n_pages // g),
            in_specs=in_specs,
            out_specs=pl.BlockSpec((1, heads * t_new, vd), lambda bi, s, pt: (bi, 0, 0)),
            scratch_shapes=[pltpu.VMEM((rows, 1), F32), pltpu.VMEM((rows, 1), F32),
                            pltpu.VMEM((rows, vd), F32)]),
        out_shape=jax.ShapeDtypeStruct((b, heads * t_new, vd), BF16),
        compiler_params=_params(("parallel", "arbitrary")),
        name="attn_sample",
    )(page_table, qs, k_new, v_new, *lam_rows, subln_row, *([cache_k] * g), *([cache_v] * g))


def _retention_kernel(lg_ref, gl_ref, q_ref, k_ref, v_ref, g_ref, *rest, length, has_state, hps):
    if has_state:
        r0_ref, nw_ref, y_ref, rout_ref, r_scr = rest
    else:
        nw_ref, y_ref, rout_ref, r_scr = rest
    hg, c = pl.program_id(1), pl.program_id(2)

    @pl.when(c == 0)
    def _():
        r_scr[...] = r0_ref[0] if has_state else jnp.zeros(r_scr.shape, F32)

    lp = q_ref.shape[1]
    dk, dv = r_scr.shape[1], r_scr.shape[2]
    rel = (lax.broadcasted_iota(jnp.int32, (lp, lp), 0)
           - lax.broadcasted_iota(jnp.int32, (lp, lp), 1)).astype(F32)
    idx = lax.broadcasted_iota(jnp.int32, (lp, 1), 0).astype(F32)
    nw = nw_ref[...]
    for hh in range(hps):
        h = hg * hps + hh
        lg = lg_ref[h]
        ks, vs = slice(hh * dk, (hh + 1) * dk), slice(hh * dv, (hh + 1) * dv)
        q, k, v = q_ref[0, :, ks], k_ref[0, :, ks], v_ref[0, :, vs]
        dmask = jnp.where(rel >= 0, jnp.exp(jnp.maximum(rel, 0.0) * lg), 0.0)
        inner = _dot_nt(q, k.astype(BF16)) * dmask
        r_old = r_scr[hh]
        o = _dot(inner.astype(BF16), v) + _dot(q, r_old.astype(BF16)) * jnp.exp((idx + 1.0) * lg)
        kdec = k * jnp.exp((length - 1.0 - idx) * lg)
        r_scr[hh] = r_old * gl_ref[h] + _dot_tn(kdec.astype(BF16), v)
        g = g_ref[0, :, vs]
        y_ref[0, :, vs] = (g * jax.nn.sigmoid(g) * _rms(o, nw)).astype(y_ref.dtype)

    @pl.when(c == pl.num_programs(2) - 1)
    def _():
        rout_ref[0] = r_scr[...]


def _retention(q, k, v, g, state, norm_row, log_g, *, heads, chunk, length, hps=8):
    b, s, _ = q.shape
    dk, dv = q.shape[2] // heads, v.shape[2] // heads
    hps = min(hps, heads)
    assert heads % hps == 0
    decay_len = jnp.exp(length * log_g)
    seq = lambda n: pl.BlockSpec((1, chunk, hps * n), lambda bi, h, c, *_: (bi, c, h))
    st = pl.BlockSpec((1, hps, dk, dv), lambda bi, h, c, *_: (bi, h, 0, 0))
    in_specs = [seq(dk), seq(dk), seq(dv), seq(dv)]
    args = [q, k, v, g]
    if state is not None:
        in_specs.append(st)
        args.append(state)
    in_specs.append(pl.BlockSpec((1, dv), lambda bi, h, c, *_: (0, 0)))
    args.append(norm_row)
    return pl.pallas_call(
        functools.partial(_retention_kernel, length=float(length), has_state=state is not None, hps=hps),
        grid_spec=pltpu.PrefetchScalarGridSpec(
            num_scalar_prefetch=2,
            grid=(b, heads // hps, s // chunk),
            in_specs=in_specs,
            out_specs=[seq(dv), st],
            scratch_shapes=[pltpu.VMEM((hps, dk, dv), F32)]),
        out_shape=[jax.ShapeDtypeStruct((b, s, heads * dv), BF16),
                   jax.ShapeDtypeStruct((b, heads, dk, dv), F32)],
        compiler_params=_params(("parallel", "parallel", "arbitrary")),
        name="retention",
    )(log_g, decay_len, *args)


def _conv_rows(u, cw_ref, cb_ref, e_scr):
    tm, pad = u.shape[0], SUBLANES
    e_scr[pad:pad + tm, :] = u
    return (cb_ref[...] + e_scr[pad - 2:pad - 2 + tm, :] * cw_ref[0:1, :]
            + e_scr[pad - 1:pad - 1 + tm, :] * cw_ref[1:2, :] + u * cw_ref[2:3, :])


def _ffn_up_kernel(a_ref, wg_ref, wu_ref, cwg_ref, cwu_ref, cbg_ref, cbu_ref, as_ref, addg_ref, addu_ref,
                   act_ref, stg_ref, stu_ref, acts_ref, extg_ref, extu_ref, eg_scr, eu_scr, esg_scr, esu_scr,
                   *, tiles_per_seq):
    i = pl.program_id(1)
    tm = a_ref.shape[0]
    pad = SUBLANES
    zeros = jnp.zeros((pad, eg_scr.shape[1]), F32)

    @pl.when(i % tiles_per_seq == 0)
    def _():
        eg_scr[0:pad, :] = zeros
        eu_scr[0:pad, :] = zeros

    wg = wg_ref[...].astype(BF16)
    wu = wu_ref[...].astype(BF16)
    a = a_ref[...]
    convs = []
    for w, cw_ref, cb_ref, e_scr, st_ref in ((wg, cwg_ref, cbg_ref, eg_scr, stg_ref),
                                            (wu, cwu_ref, cbu_ref, eu_scr, stu_ref)):
        convs.append(_conv_rows(_dot(a, w), cw_ref, cb_ref, e_scr))
        st_ref[0] = e_scr[pad + tm - 2:pad + tm, :]
        e_scr[0:pad, :] = e_scr[tm:tm + pad, :]
    gate, up = convs
    act_ref[...] = (gate * jax.nn.sigmoid(gate) * up).astype(act_ref.dtype)

    @pl.when(i == 0)
    def _():
        a_s = as_ref[...]
        convs = []
        for w, cw_ref, cb_ref, add_ref, e_scr, ext_ref in ((wg, cwg_ref, cbg_ref, addg_ref, esg_scr, extg_ref),
                                                          (wu, cwu_ref, cbu_ref, addu_ref, esu_scr, extu_ref)):
            e_scr[0:pad, :] = zeros
            u = _dot(a_s, w) + add_ref[...]
            ext_ref[...] = u
            convs.append(_conv_rows(u, cw_ref, cb_ref, e_scr))
        gate_s, up_s = convs
        acts_ref[...] = (gate_s * jax.nn.sigmoid(gate_s) * up_s).astype(acts_ref.dtype)


def _ffn_up(a, a_s, add_s, w_up, conv_w, conv_b, layer, *, d_ff, seq_len, tm=1024, tn=256):
    m, kdim = a.shape
    ms = a_s.shape[0]
    tm, tn = min(tm, m, seq_len), min(tn, d_ff)
    assert m % tm == 0 and d_ff % tn == 0 and seq_len % tm == 0
    nf = d_ff // tn
    tps = seq_len // tm
    wspec = lambda off: pl.BlockSpec((None, kdim, tn), lambda j, i: (layer, 0, off + j))
    cwspec = lambda off: pl.BlockSpec((conv_w.shape[0], tn), lambda j, i: (0, off + j))
    cbspec = lambda off: pl.BlockSpec((1, tn), lambda j, i: (0, off + j))
    sspec = lambda off: pl.BlockSpec((ms, tn), lambda j, i: (0, off + j))
    st_spec = pl.BlockSpec((1, 2, tn), lambda j, i: (i // tps, 0, j))
    in_specs = [pl.BlockSpec((tm, kdim), lambda j, i: (i, 0)), wspec(0), wspec(nf),
                cwspec(0), cwspec(nf), cbspec(0), cbspec(nf),
                pl.BlockSpec((ms, kdim), lambda j, i: (0, 0)), sspec(0), sspec(nf)]
    state = jax.ShapeDtypeStruct((m // seq_len, 2, d_ff), F32)
    ext = jax.ShapeDtypeStruct((ms, d_ff), F32)
    return pl.pallas_call(
        functools.partial(_ffn_up_kernel, tiles_per_seq=tps),
        grid=(nf, m // tm),
        in_specs=in_specs,
        out_specs=[pl.BlockSpec((tm, tn), lambda j, i: (i, j)), st_spec, st_spec, sspec(0), sspec(0), sspec(0)],
        out_shape=[jax.ShapeDtypeStruct((m, d_ff), BF16), state, state,
                   jax.ShapeDtypeStruct((ms, d_ff), BF16), ext, ext],
        scratch_shapes=[pltpu.VMEM((tm + SUBLANES, tn), F32), pltpu.VMEM((tm + SUBLANES, tn), F32),
                        pltpu.VMEM((ms + SUBLANES, tn), F32), pltpu.VMEM((ms + SUBLANES, tn), F32)],
        compiler_params=_params(("parallel", "arbitrary")),
        name="ffn_up",
    )(a, w_up, w_up, conv_w, conv_w, conv_b, conv_b, a_s, add_s, add_s)


def _lambda_init(layer_idx):
    return 0.8 - 0.6 * math.exp(-0.3 * layer_idx)


def _rotation_tables(pos, dk, scale):
    inv = 1.0 / (RET_ROT_BASE ** jnp.linspace(0.0, 1.0, dk // 2, dtype=F32))
    ang = pos.astype(F32)[:, None] * inv[None, :]
    cos, sin = jnp.cos(ang), jnp.sin(ang)
    cos_full = jnp.stack([cos, cos], axis=-1).reshape(pos.shape[0], dk)
    sin_signed = jnp.stack([-sin, sin], axis=-1).reshape(pos.shape[0], dk)
    return cos_full * scale, sin_signed * scale


def _ffn_down(x, act, layer, w_down_bf16):
    d_ff = w_down_bf16.shape[1]
    tk = d_ff // 2 if d_ff > 4096 else d_ff
    (y,) = _matmul(act, w_down_bf16, layer, col0=0, ncols=x.shape[1], out_dtypes=[F32],
                   epilogue=_epi_residual, extras=[x], extra_specs=[_tile_spec(min(1024, x.shape[0]), 512)],
                   tm=1024, tn=512, tk=tk, name="ffn_down")
    return y


def kernel(x_prompt, x_sample, cache_k, cache_v, state_ret, state_ffn, page_table, norm_mix_w, norm_ffn_w, w_attn_qkv, q_norm_w, k_norm_w, lambda_q1, lambda_k1, lambda_q2, lambda_k2, subln_w, w_attn_o, w_ret_qkvg, ret_norm_w, w_ret_o, w_ffn_up, ffn_conv_w, ffn_conv_b, w_ffn_down):
    bp, sp, d = x_prompt.shape
    bs, ss, _ = x_sample.shape
    depth = norm_mix_w.shape[0]
    hd, vd = q_norm_w.shape[-1], subln_w.shape[-1]
    sub, heads = cache_k.shape[3], cache_v.shape[3]
    qw, vw = sub * hd, heads * vd
    past = page_table.shape[1] * cache_k.shape[2]
    rheads, dk, dv = state_ret.shape[2], state_ret.shape[3], state_ret.shape[4]
    rqk, rv = rheads * dk, rheads * dv
    d_ff = w_ffn_down.shape[1]
    conv_taps = ffn_conv_w.shape[1] - 1
    row = lambda w: w[None, :]

    w_down = _to_bf16(w_ffn_down)
    w_ret_o_bf16 = _to_bf16(w_ret_o)
    xp = x_prompt.reshape(bp * sp, d)
    xs = x_sample.reshape(bs * ss, d)
    kp_l, vp_l, ks_l, vs_l, rp_l, rs_l, fp_l, fs_l = [], [], [], [], [], [], [], []
    for i in range(depth):
        hp = _rmsnorm(xp, row(norm_mix_w[i]))
        hs = _rmsnorm(xs, row(norm_mix_w[i]))
        if i % 2 == 0:
            a = i // 2
            lam_init = _lambda_init(i)
            lam_rows = [row(lambda_q1[a]), row(lambda_k1[a]), row(lambda_q2[a]), row(lambda_k2[a])]
            knorm = functools.partial(_epi_headnorm, head_dim=hd, scale=1.0)
            nspec = [pl.BlockSpec((1, hd), lambda i_, j, k: (0, 0))]

            def qkv(h_rows, q_scale):
                qnorm = functools.partial(_epi_headnorm, head_dim=hd, scale=q_scale)
                (q,) = _matmul(h_rows, w_attn_qkv, a, col0=0, ncols=qw, out_dtypes=[BF16], epilogue=qnorm,
                               extras=[row(q_norm_w[a])], extra_specs=nspec, name="attn_q")
                k32, k16 = _matmul(h_rows, w_attn_qkv, a, col0=qw, ncols=qw, out_dtypes=[F32, BF16],
                                   epilogue=knorm, extras=[row(k_norm_w[a])], extra_specs=nspec,
                                   name="attn_k")
                v32, v16 = _matmul(h_rows, w_attn_qkv, a, col0=2 * qw, ncols=vw, out_dtypes=[F32, BF16],
                                   epilogue=_epi_store, name="attn_v")
                return q, k32, k16, v32, v16

            def out_proj(x, o_rows):
                (y,) = _matmul(o_rows, w_attn_o, a, col0=0, ncols=d, out_dtypes=[F32],
                               epilogue=_epi_residual, extras=[x],
                               extra_specs=[_tile_spec(min(MM_TM, x.shape[0]), MM_TN)], name="attn_o")
                return y

            q, k32, k16, v32, v16 = qkv(hp, hd ** -0.5 * LOG2E)
            op = _attn_prompt(q.reshape(bp, sp, qw), k16.reshape(bp, sp, qw), v16.reshape(bp, sp, vw),
                              lam_rows, row(subln_w[a]), heads=heads, head_dim=hd, v_dim=vd,
                              lam_init=lam_init)
            xp = out_proj(xp, op.reshape(bp * sp, vw))
            kp_l.append(k32.reshape(bp, sp, sub, hd))
            vp_l.append(v32.reshape(bp, sp, heads, vd))

            q, k32, _, v32, _ = qkv(hs, hd ** -0.5)
            k_new = k32.reshape(bs, ss, sub, hd)
            v_new = v32.reshape(bs, ss, heads, vd)
            qs = q.reshape(bs, ss, heads, 2, hd).transpose(0, 3, 2, 1, 4).reshape(bs, 2, heads * ss, hd)
            osm = _attn_sample(qs, k_new, v_new, cache_k, cache_v, a, page_table, lam_rows,
                               row(subln_w[a]), lam_init=lam_init)
            osm = osm.reshape(bs, heads, ss, vd).transpose(0, 2, 1, 3).reshape(bs * ss, vw)
            xs = out_proj(xs, osm)
            ks_l.append(k_new)
            vs_l.append(v_new)
        else:
            r = i // 2
            log_g = jnp.log1p(-jnp.exp2(-5.0 - jnp.arange(rheads, dtype=F32)))

            def project(h_rows, cos_q, sin_q, cos_k, sin_k, tm):
                reps = MM_TN // dk
                tabs = [jnp.tile(t, (1, reps)) for t in (cos_q, sin_q, cos_k, sin_k)]
                nrep = tabs[0].shape[0] // tm
                tspec = [pl.BlockSpec((tm, MM_TN), lambda i_, j, k: (i_ % nrep, 0))] * 2
                (q,) = _matmul(h_rows, w_ret_qkvg, r, col0=0, ncols=rqk, out_dtypes=[BF16],
                               epilogue=_epi_rotate, extras=tabs[:2], extra_specs=tspec, tm=tm, name="ret_q")
                (k,) = _matmul(h_rows, w_ret_qkvg, r, col0=rqk, ncols=rqk, out_dtypes=[F32],
                               epilogue=_epi_rotate, extras=tabs[2:], extra_specs=tspec, tm=tm, name="ret_k")
                (v,) = _matmul(h_rows, w_ret_qkvg, r, col0=2 * rqk, ncols=rv, out_dtypes=[BF16],
                               epilogue=_epi_store, tm=tm, name="ret_v")
                (g,) = _matmul(h_rows, w_ret_qkvg, r, col0=2 * rqk + rv, ncols=rv, out_dtypes=[F32],
                               epilogue=_epi_store, tm=tm, name="ret_g")
                return q, k, v, g

            def out_proj(x, y_rows):
                (y,) = _matmul(y_rows, w_ret_o_bf16, r, col0=0, ncols=d, out_dtypes=[F32],
                               epilogue=_epi_residual, extras=[x],
                               extra_specs=[_tile_spec(min(1024, x.shape[0]), MM_TN)],
                               tm=1024, tk=rv, name="ret_o")
                return y

            tm = min(MM_TM, sp)
            cq, sq = _rotation_tables(jnp.arange(sp), dk, 1.0)
            ck, sk = _rotation_tables(jnp.arange(sp), dk, dk ** -0.5)
            q, k, v, g = project(hp, cq, sq, ck, sk, tm)
            chunk = min(128, sp)
            y, rp = _retention(q.reshape(bp, sp, rqk), k.reshape(bp, sp, rqk), v.reshape(bp, sp, rv),
                               g.reshape(bp, sp, rv), None, row(ret_norm_w[r]), log_g,
                               heads=rheads, chunk=chunk, length=chunk)
            xp = out_proj(xp, y.reshape(bp * sp, rv))
            rp_l.append(rp)

            pos_s = past + jnp.arange(ss)
            cq, sq = (jnp.tile(t, (bs, 1)) for t in _rotation_tables(pos_s, dk, 1.0))
            ck, sk = (jnp.tile(t, (bs, 1)) for t in _rotation_tables(pos_s, dk, dk ** -0.5))
            q, k, v, g = project(hs, cq, sq, ck, sk, bs * ss)
            chunk = 128
            padded = lambda t: jnp.pad(t.reshape(bs, ss, -1), ((0, 0), (0, chunk - ss), (0, 0)))
            y, rs = _retention(padded(q), padded(k), padded(v), padded(g), state_ret[r],
                               row(ret_norm_w[r]), log_g, heads=rheads, chunk=chunk, length=ss)
            xs = out_proj(xs, y[:, :ss].reshape(bs * ss, rv))
            rs_l.append(rs)

        group = BF16_ROWS
        assert conv_taps + ss <= group
        hp = _rmsnorm(xp, row(norm_ffn_w[i]))
        hs = _rmsnorm(xs, row(norm_ffn_w[i]))
        hs_pad = jnp.pad(hs.reshape(bs, ss, d), ((0, 0), (conv_taps, group - conv_taps - ss), (0, 0)))
        add = jnp.pad(state_ffn[i], ((0, 0), (0, group - conv_taps), (0, 0)))
        xs_pad = jnp.pad(xs.reshape(bs, ss, d), ((0, 0), (conv_taps, group - conv_taps - ss), (0, 0)))
        act_p, st_g, st_u, act_s, ext_g, ext_u = _ffn_up(
            hp, hs_pad.reshape(bs * group, d), add.reshape(bs * group, 2 * d_ff), w_ffn_up,
            ffn_conv_w[i], ffn_conv_b[i][None, :], i, d_ff=d_ff, seq_len=sp)
        xp = _ffn_down(xp, act_p, i, w_down)
        fp_l.append(jnp.concatenate([st_g, st_u], axis=-1))
        ys_pad = _ffn_down(xs_pad.reshape(bs * group, d), act_s, i, w_down)
        xs = ys_pad.reshape(bs, group, d)[:, conv_taps:conv_taps + ss].reshape(bs * ss, d)
        ext = jnp.concatenate([ext_g, ext_u], axis=-1).reshape(bs, group, 2 * d_ff)
        fs_l.append(ext[:, ss:ss + conv_taps])

    return (xp.reshape(bp, sp, d), xs.reshape(bs, ss, d),
            jnp.stack(kp_l), jnp.stack(vp_l), jnp.stack(ks_l), jnp.stack(vs_l),
            jnp.stack(rp_l), jnp.stack(rs_l), jnp.stack(fp_l), jnp.stack(fs_l))
```
